```python
import math
import jax, jax.numpy as jnp
from jax import lax
import numpy as np

D_MODEL = 1024
BATCH = 4
SEQ = 8192
DEPTH = 2

N_MIXERS = 2
N_MLSTM_LAYERS = (DEPTH + 1) // 2
N_CONV_LAYERS = DEPTH // 2

M_HEADS = 4
M_DV = D_MODEL // M_HEADS
M_DQK = M_DV // 2
M_QK_WIDTH = 2 * M_HEADS * M_DQK
M_V_WIDTH = M_HEADS * M_DV
M_IN_WIDTH = M_QK_WIDTH + M_V_WIDTH + M_V_WIDTH + 2 * M_HEADS
M_CONV_K = 4
M_CHUNK = 64
FGATE_BIAS = 3.0

C_KERNEL = 31

D_FF = 2816
F_CONV_K = 3

EPS = 1e-6

kernel_name = "hybrid_mlstm_conformer_convffn"


def rmsnorm(x, g):
    xf = x.astype(jnp.float32)
    y = xf * lax.rsqrt(jnp.mean(xf * xf, axis=-1, keepdims=True) + EPS)
    return (y * g.astype(jnp.float32)).astype(x.dtype)


def layernorm(x, g, b):
    xf = x.astype(jnp.float32)
    mu = jnp.mean(xf, axis=-1, keepdims=True)
    var = jnp.mean(jnp.square(xf - mu), axis=-1, keepdims=True)
    y = (xf - mu) * lax.rsqrt(var + EPS)
    return (y * g.astype(jnp.float32) + b.astype(jnp.float32)).astype(x.dtype)


def causal_dwconv(x, w, b):
    k, c = w.shape
    y = lax.conv_general_dilated(
        x, w[:, None, :].astype(x.dtype), window_strides=(1,), padding=[(k - 1, 0)],
        dimension_numbers=("NWC", "WIO", "NWC"), feature_group_count=c)
    return y + b.astype(x.dtype)


def mlstm_chunkwise(q, k, v, logi, logf):
    bsz, nh, t, dk = q.shape
    dv = v.shape[-1]
    nc = t // M_CHUNK

    def chunks(a):
        a = a.reshape(a.shape[:2] + (nc, M_CHUNK) + a.shape[3:])
        return jnp.moveaxis(a, 2, 0)

    qc, kc, vc = chunks(q), chunks(k), chunks(v)
    lic = chunks(logi)
    bc = jnp.cumsum(chunks(logf), axis=-1)
    causal = jnp.tril(jnp.ones((M_CHUNK, M_CHUNK), dtype=bool))

    def step(carry, inp):
        c_st, n_st, m_st = carry
        qb, kb, vb, li, bb = inp
        d = bb[..., :, None] - bb[..., None, :] + li[..., None, :]
        d = jnp.where(causal, d, -jnp.inf)
        inter = bb + m_st[..., None]
        m_t = jnp.maximum(inter, jnp.max(d, axis=-1))
        s = jnp.einsum("bhtd,bhsd->bhts", qb, kb) * jnp.exp(d - m_t[..., None])
        sc = jnp.exp(inter - m_t)
        num = jnp.einsum("bhts,bhsv->bhtv", s, vb) + sc[..., None] * jnp.einsum("bhvd,bhtd->bhtv", c_st, qb)
        den = jnp.sum(s, axis=-1) + sc * jnp.einsum("bhd,bhtd->bht", n_st, qb)
        h = num / jnp.maximum(jnp.abs(den), jnp.exp(-m_t))[..., None]
        b_last = bb[..., -1]
        w_log = b_last[..., None] - bb + li
        m_new = jnp.maximum(b_last + m_st, jnp.max(w_log, axis=-1))
        decay = jnp.exp(b_last + m_st - m_new)
        ws = jnp.exp(w_log - m_new[..., None])
        c_new = decay[..., None, None] * c_st + jnp.einsum("bhs,bhsv,bhsd->bhvd", ws, vb, kb)
        n_new = decay[..., None] * n_st + jnp.einsum("bhs,bhsd->bhd", ws, kb)
        return (c_new, n_new, m_new), h

    init = (jnp.zeros((bsz, nh, dv, dk), jnp.float32),
            jnp.zeros((bsz, nh, dk), jnp.float32),
            jnp.zeros((bsz, nh), jnp.float32))
    _, hs = lax.scan(step, init, (qc, kc, vc, lic, bc))
    return jnp.moveaxis(hs, 0, 2).reshape(bsz, nh, t, dv)


def mlstm_mixer(x, w_in, conv_w, conv_b, b_gates, head_norm, w_out):
    bsz, t, _ = x.shape
    proj = x @ w_in
    qk = jax.nn.silu(causal_dwconv(proj[..., :M_QK_WIDTH], conv_w, conv_b))
    v = proj[..., M_QK_WIDTH:M_QK_WIDTH + M_V_WIDTH]
    o = proj[..., M_QK_WIDTH + M_V_WIDTH:M_QK_WIDTH + 2 * M_V_WIDTH]
    gates = (proj[..., M_QK_WIDTH + 2 * M_V_WIDTH:] + b_gates).astype(jnp.float32)
    logi = jnp.transpose(gates[..., :M_HEADS], (0, 2, 1))
    logf = jnp.transpose(jax.nn.log_sigmoid(gates[..., M_HEADS:]), (0, 2, 1))

    def heads(a, d):
        return jnp.transpose(a.reshape(bsz, t, M_HEADS, d), (0, 2, 1, 3)).astype(jnp.float32)

    q = heads(qk[..., :M_HEADS * M_DQK], M_DQK) * (M_DQK ** -0.5)
    k = heads(qk[..., M_HEADS * M_DQK:], M_DQK)
    h = mlstm_chunkwise(q, k, heads(v, M_DV), logi, logf)
    h = jnp.transpose(h, (0, 2, 1, 3))
    h = h * lax.rsqrt(jnp.mean(h * h, axis=-1, keepdims=True) + EPS)
    h = h.reshape(bsz, t, M_V_WIDTH) * head_norm.astype(jnp.float32)
    h = h.astype(x.dtype) * jax.nn.sigmoid(o)
    return h @ w_out


def conformer_conv_mixer(x, w_in, b_in, dw_w, dw_b, ln_g, ln_b, w_out, b_out):
    a = x @ w_in + b_in
    u = a[..., :D_MODEL] * jax.nn.sigmoid(a[..., D_MODEL:])
    u = causal_dwconv(u, dw_w, dw_b)
    u = jax.nn.silu(layernorm(u, ln_g, ln_b))
    return u @ w_out + b_out


def conv_ffn(x, w_up, conv_w, conv_b, w_down):
    hdn = causal_dwconv(x @ w_up, conv_w, conv_b)
    return (jax.nn.silu(hdn[..., :D_FF]) * hdn[..., D_FF:]) @ w_down


def setup_inputs(seed: int = 0) -> dict:
    key = jax.random.key(seed)
    ks = iter(jax.random.split(key, 32))
    f32 = jnp.float32
    nrm = lambda shape, s: jax.random.normal(next(ks), shape, f32) * s
    out_scale = 1.0 / math.sqrt(2 * DEPTH)
    nm, ncv = N_MLSTM_LAYERS, N_CONV_LAYERS
    gate_bias = jnp.concatenate([jnp.zeros((nm, M_HEADS), f32), jnp.full((nm, M_HEADS), FGATE_BIAS, f32)], axis=-1)
    return {
        "x": nrm((BATCH, SEQ, D_MODEL), 1.0),
        "norm_mix": 1.0 + nrm((DEPTH, D_MODEL), 0.02),
        "norm_ffn": 1.0 + nrm((DEPTH, D_MODEL), 0.02),
        "norm_final": 1.0 + nrm((D_MODEL,), 0.02),
        "m_w_in": nrm((nm, D_MODEL, M_IN_WIDTH), D_MODEL ** -0.5),
        "m_conv_w": nrm((nm, M_CONV_K, M_QK_WIDTH), M_CONV_K ** -0.5),
        "m_conv_b": nrm((nm, M_QK_WIDTH), 0.02),
        "m_b_gates": gate_bias + nrm((nm, 2 * M_HEADS), 0.1),
        "m_head_norm": 1.0 + nrm((nm, M_V_WIDTH), 0.02),
        "m_w_out": nrm((nm, M_V_WIDTH, D_MODEL), M_V_WIDTH ** -0.5 * out_scale),
        "c_w_in": nrm((ncv, D_MODEL, 2 * D_MODEL), D_MODEL ** -0.5),
        "c_b_in": nrm((ncv, 2 * D_MODEL), 0.02),
        "c_dw_w": nrm((ncv, C_KERNEL, D_MODEL), C_KERNEL ** -0.5),
        "c_dw_b": nrm((ncv, D_MODEL), 0.02),
        "c_ln_g": 1.0 + nrm((ncv, D_MODEL), 0.02),
        "c_ln_b": nrm((ncv, D_MODEL), 0.02),
        "c_w_out": nrm((ncv, D_MODEL, D_MODEL), D_MODEL ** -0.5 * out_scale),
        "c_b_out": nrm((ncv, D_MODEL), 0.02),
        "f_w_up": nrm((DEPTH, D_MODEL, 2 * D_FF), D_MODEL ** -0.5),
        "f_conv_w": nrm((DEPTH, F_CONV_K, 2 * D_FF), F_CONV_K ** -0.5),
        "f_conv_b": nrm((DEPTH, 2 * D_FF), 0.02),
        "f_w_down": nrm((DEPTH, D_FF, D_MODEL), D_FF ** -0.5 * out_scale),
    }


def reference(x, norm_mix, norm_ffn, norm_final,
              m_w_in, m_conv_w, m_conv_b, m_b_gates, m_head_norm, m_w_out,
              c_w_in, c_b_in, c_dw_w, c_dw_b, c_ln_g, c_ln_b, c_w_out, c_b_out,
              f_w_up, f_conv_w, f_conv_b, f_w_down):
    for i in range(DEPTH):
        h = rmsnorm(x, norm_mix[i])
        j = i // N_MIXERS
        if i % N_MIXERS == 0:
            x = x + mlstm_mixer(h, m_w_in[j], m_conv_w[j], m_conv_b[j], m_b_gates[j],
                                m_head_norm[j], m_w_out[j])
        else:
            x = x + conformer_conv_mixer(h, c_w_in[j], c_b_in[j], c_dw_w[j], c_dw_b[j],
                                         c_ln_g[j], c_ln_b[j], c_w_out[j], c_b_out[j])
        x = x + conv_ffn(rmsnorm(x, norm_ffn[i]), f_w_up[i], f_conv_w[i], f_conv_b[i], f_w_down[i])
    return rmsnorm(x, norm_final)
```

```python
import functools

import jax
import jax.numpy as jnp
from jax import lax
from jax.experimental import pallas as pl
from jax.experimental.pallas import tpu as pltpu

EPS = 1e-6
M_HEADS = 4
FGATE_ROWS = 8
SUBLANES = 8
V7X_VMEM_BYTES = 64 * 1024 * 1024
VMEM_LIMIT_BYTES = 56 * 1024 * 1024

F32 = jnp.float32
BF16 = jnp.bfloat16


def _rms(x, g):
    return x * lax.rsqrt(jnp.mean(x * x, axis=-1, keepdims=True) + EPS) * g


def _resident(shape):
    return pl.BlockSpec(shape, lambda b, i: (0,) * len(shape), pipeline_mode=pl.Buffered(1))


def _row_tile(tm, width):
    return pl.BlockSpec((None, tm, width), lambda b, i: (b, i, 0))


def _params():
    return pltpu.CompilerParams(dimension_semantics=("arbitrary", "arbitrary"),
                                vmem_limit_bytes=VMEM_LIMIT_BYTES)


def _causal_taps(u, u_ref, halo_ref, cols, w, halo_rows):
    tm = u.shape[0]
    k = w.shape[0]
    u_ref[0:halo_rows, :] = halo_ref[:, cols]
    u_ref[halo_rows:halo_rows + tm, :] = u
    halo_ref[:, cols] = u[tm - halo_rows:, :]
    y = w[k - 1:k, :] * u
    for j in range(k - 1):
        start = halo_rows + j - (k - 1)
        y = y + w[j:j + 1, :] * u_ref[start:start + tm, :]
    return y


def _ffn_kernel(x_ref, g_ref, wup_ref, cw_ref, cb_ref, wdn_ref, gfin_ref, o_ref,
                halo_ref, u_ref, acc_ref, *, d_ff, tf, final_norm):
    @pl.when(pl.program_id(1) == 0)
    def _():
        halo_ref[...] = jnp.zeros_like(halo_ref)

    x = x_ref[...]
    hn = _rms(x, g_ref[...]).astype(BF16)
    for c in range(d_ff // tf):
        ys = []
        for half in range(2):
            cols = slice(half * d_ff + c * tf, half * d_ff + (c + 1) * tf)
            u = jnp.dot(hn, wup_ref[:, cols], preferred_element_type=F32)
            y = _causal_taps(u, u_ref.at[half], halo_ref, cols, cw_ref[:, cols], SUBLANES)
            ys.append(y + cb_ref[:, cols])
        act = (ys[0] * jax.nn.sigmoid(ys[0]) * ys[1]).astype(BF16)
        contrib = jnp.dot(act, wdn_ref[c * tf:(c + 1) * tf, :], preferred_element_type=F32)
        if c == 0:
            acc_ref[...] = contrib
        else:
            acc_ref[...] += contrib
    out = x + acc_ref[...]
    if final_norm:
        out = _rms(out, gfin_ref[...])
    o_ref[...] = out


def _conv_ffn(x, g, w_up, conv_w, conv_b, w_down, g_final, *, final_norm, tm=512, tf=256):
    bsz, t, d = x.shape
    d_ff = w_down.shape[0]
    kern = functools.partial(_ffn_kernel, d_ff=d_ff, tf=tf, final_norm=final_norm)
    return pl.pallas_call(
        kern,
        grid=(bsz, t // tm),
        in_specs=[_row_tile(tm, d), _resident((1, d)), _resident((d, 2 * d_ff)),
                  _resident(conv_w.shape), _resident((1, 2 * d_ff)), _resident((d_ff, d)),
                  _resident((1, d))],
        out_specs=_row_tile(tm, d),
        out_shape=jax.ShapeDtypeStruct(x.shape, F32),
        scratch_shapes=[pltpu.VMEM((SUBLANES, 2 * d_ff), F32),
                        pltpu.VMEM((2, SUBLANES + tm, tf), F32),
                        pltpu.VMEM((tm, d), F32)],
        compiler_params=_params(),
        name="conv_ffn",
    )(x, g.reshape(1, d), w_up.astype(BF16), conv_w, conv_b.reshape(1, -1),
      w_down.astype(BF16), g_final.reshape(1, d))


def _conformer_kernel(x_ref, g_ref, win_ref, bin_ref, dw_ref, dwb_ref, lng_ref, lnb_ref,
                      wout_ref, bout_ref, o_ref, halo_ref, u_ref, *, halo_rows):
    @pl.when(pl.program_id(1) == 0)
    def _():
        halo_ref[...] = jnp.zeros_like(halo_ref)

    x = x_ref[...]
    d = x.shape[1]
    hn = _rms(x, g_ref[...]).astype(BF16)
    a1 = jnp.dot(hn, win_ref[:, 0:d], preferred_element_type=F32) + bin_ref[:, 0:d]
    a2 = jnp.dot(hn, win_ref[:, d:2 * d], preferred_element_type=F32) + bin_ref[:, d:2 * d]
    u = a1 * jax.nn.sigmoid(a2)
    y = _causal_taps(u, u_ref, halo_ref, slice(0, d), dw_ref[...], halo_rows) + dwb_ref[...]
    mu = jnp.mean(y, axis=-1, keepdims=True)
    yc = y - mu
    var = jnp.mean(yc * yc, axis=-1, keepdims=True)
    z = yc * lax.rsqrt(var + EPS) * lng_ref[...] + lnb_ref[...]
    z = (z * jax.nn.sigmoid(z)).astype(BF16)
    o_ref[...] = x + jnp.dot(z, wout_ref[...], preferred_element_type=F32) + bout_ref[...]


def _conformer(x, g, w_in, b_in, dw_w, dw_b, ln_g, ln_b, w_out, b_out, *, tm=512):
    bsz, t, d = x.shape
    k = dw_w.shape[0]
    halo_rows = -(-(k - 1) // SUBLANES) * SUBLANES
    kern = functools.partial(_conformer_kernel, halo_rows=halo_rows)
    vec = lambda a: a.reshape(1, -1)
    return pl.pallas_call(
        kern,
        grid=(bsz, t // tm),
        in_specs=[_row_tile(tm, d), _resident((1, d)), _resident((d, 2 * d)),
                  _resident((1, 2 * d)), _resident((k, d)), _resident((1, d)),
                  _resident((1, d)), _resident((1, d)), _resident((d, d)), _resident((1, d))],
        out_specs=_row_tile(tm, d),
        out_shape=jax.ShapeDtypeStruct(x.shape, F32),
        scratch_shapes=[pltpu.VMEM((halo_rows, d), F32),
                        pltpu.VMEM((halo_rows + tm, d), F32)],
        compiler_params=_params(),
        name="conformer",
    )(x, vec(g), w_in.astype(BF16), vec(b_in), dw_w, vec(dw_b), vec(ln_g), vec(ln_b),
      w_out.astype(BF16), vec(b_out))


def _mlstm_proj_kernel(x_ref, g_ref, wqk_ref, wv_ref, wo_ref, wgt_ref, cw_ref, cb_ref, bg_ref,
                       qk_ref, v_ref, o_ref, gt_ref, halo_ref, u_ref, *, q_scale):
    @pl.when(pl.program_id(1) == 0)
    def _():
        halo_ref[...] = jnp.zeros_like(halo_ref)

    x = x_ref[...]
    hn = _rms(x, g_ref[...]).astype(BF16)
    width = wqk_ref.shape[1]
    pre = jnp.dot(hn, wqk_ref[...], preferred_element_type=F32)
    y = _causal_taps(pre, u_ref, halo_ref, slice(0, width), cw_ref[...], SUBLANES) + cb_ref[...]
    y = y * jax.nn.sigmoid(y)
    col = lax.broadcasted_iota(jnp.int32, (1, width), 1)
    qk_ref[...] = (y * jnp.where(col < width // 2, q_scale, 1.0)).astype(BF16)
    v_ref[...] = jnp.dot(hn, wv_ref[...], preferred_element_type=F32).astype(BF16)
    o_ref[...] = jnp.dot(hn, wo_ref[...], preferred_element_type=F32).astype(BF16)
    gt = lax.dot_general(wgt_ref[...], hn, (((1,), (1,)), ((), ())),
                         preferred_element_type=F32) + bg_ref[...]
    row = lax.broadcasted_iota(jnp.int32, gt.shape, 0)
    gt_ref[...] = jnp.where(row < M_HEADS, gt, jax.nn.log_sigmoid(gt))


def _mlstm_core_kernel(x_ref, qk_ref, v_ref, o_ref, gt_ref, hnorm_ref, wout_ref, out_ref,
                       ct_ref, n_ref, m_ref, *, dqk, dv):
    @pl.when(pl.program_id(1) == 0)
    def _():
        ct_ref[...] = jnp.zeros_like(ct_ref)
        n_ref[...] = jnp.zeros_like(n_ref)
        m_ref[...] = jnp.zeros_like(m_ref)

    ln = x_ref.shape[0]
    row = lax.broadcasted_iota(jnp.int32, (ln, ln), 0)
    col = lax.broadcasted_iota(jnp.int32, (ln, ln), 1)
    tri = col <= row
    eye = col == row
    heads = []
    for h in range(M_HEADS):
        q = qk_ref[:, h * dqk:(h + 1) * dqk]
        k = qk_ref[:, (M_HEADS + h) * dqk:(M_HEADS + h + 1) * dqk]
        v = v_ref[:, h * dv:(h + 1) * dv]
        li_r = gt_ref[h:h + 1, :]
        lf_r = gt_ref[M_HEADS + h:M_HEADS + h + 1, :]
        bb_c = jnp.sum(jnp.where(tri, lf_r, 0.0), axis=1, keepdims=True)
        bb_r = jnp.sum(jnp.where(eye, bb_c, 0.0), axis=0, keepdims=True)
        li_c = jnp.sum(jnp.where(eye, li_r, 0.0), axis=1, keepdims=True)
        m_st = m_ref[h][:, 0:1]
        n_st = n_ref[h]
        ct = ct_ref[h]
        dmat = jnp.where(tri, bb_c + (li_r - bb_r), -jnp.inf)
        inter = bb_c + m_st
        m_t = jnp.maximum(inter, jnp.max(dmat, axis=1, keepdims=True))
        s = lax.dot_general(q, k, (((1,), (1,)), ((), ())), preferred_element_type=F32)
        s = s * jnp.exp(dmat - m_t)
        sc = jnp.exp(inter - m_t)
        num = (jnp.dot(s.astype(BF16), v, preferred_element_type=F32)
               + sc * jnp.dot(q, ct.astype(BF16), preferred_element_type=F32))
        den = (jnp.sum(s, axis=1, keepdims=True)
               + sc * jnp.sum(q.astype(F32) * n_st, axis=1, keepdims=True))
        hh = num / jnp.maximum(jnp.abs(den), jnp.exp(-m_t))
        b_last = bb_c[ln - 1:ln, :]
        wlog = b_last - bb_c + li_c
        m_new = jnp.maximum(b_last + m_st, jnp.max(wlog, axis=0, keepdims=True))
        decay = jnp.exp(b_last + m_st - m_new)
        ws = jnp.exp(wlog - m_new)
        wv = (ws * v.astype(F32)).astype(BF16)
        ct_ref[h] = decay * ct + lax.dot_general(k, wv, (((0,), (0,)), ((), ())),
                                                 preferred_element_type=F32)
        n_ref[h] = decay * n_st + jnp.sum(ws * k.astype(F32), axis=0, keepdims=True)
        m_ref[h] = jnp.broadcast_to(m_new, m_ref.shape[1:])
        heads.append(hh * lax.rsqrt(jnp.mean(hh * hh, axis=-1, keepdims=True) + EPS))
    hcat = jnp.concatenate(heads, axis=-1) * hnorm_ref[...]
    gated = (hcat * jax.nn.sigmoid(o_ref[...].astype(F32))).astype(BF16)
    out_ref[...] = x_ref[...] + jnp.dot(gated, wout_ref[...], preferred_element_type=F32)


def _mlstm_layer(x, g, w_in, conv_w, conv_b, b_gates, head_norm, w_out, *, tm=512, chunk=256):
    bsz, t, d = x.shape
    qk_w = conv_w.shape[1]
    v_w = head_norm.shape[0]
    dqk = qk_w // (2 * M_HEADS)
    dv = v_w // M_HEADS
    w_in = w_in.astype(BF16)
    w_qk = w_in[:, :qk_w]
    w_v = w_in[:, qk_w:qk_w + v_w]
    w_o = w_in[:, qk_w + v_w:qk_w + 2 * v_w]
    w_gt = w_in[:, qk_w + 2 * v_w:].T
    kern = functools.partial(_mlstm_proj_kernel, q_scale=dqk ** -0.5)
    qk, v, o, gt = pl.pallas_call(
        kern,
        grid=(bsz, t // tm),
        in_specs=[_row_tile(tm, d), _resident((1, d)), _resident((d, qk_w)),
                  _resident((d, v_w)), _resident((d, v_w)), _resident((FGATE_ROWS, d)),
                  _resident(conv_w.shape), _resident((1, qk_w)), _resident((FGATE_ROWS, 1))],
        out_specs=[_row_tile(tm, qk_w), _row_tile(tm, v_w), _row_tile(tm, v_w),
                   pl.BlockSpec((None, FGATE_ROWS, tm), lambda b, i: (b, 0, i))],
        out_shape=[jax.ShapeDtypeStruct((bsz, t, qk_w), BF16),
                   jax.ShapeDtypeStruct((bsz, t, v_w), BF16),
                   jax.ShapeDtypeStruct((bsz, t, v_w), BF16),
                   jax.ShapeDtypeStruct((bsz, FGATE_ROWS, t), F32)],
        scratch_shapes=[pltpu.VMEM((SUBLANES, qk_w), F32),
                        pltpu.VMEM((SUBLANES + tm, qk_w), F32)],
        compiler_params=_params(),
        name="mlstm_proj",
    )(x, g.reshape(1, d), w_qk, w_v, w_o, w_gt, conv_w, conv_b.reshape(1, -1),
      b_gates.reshape(FGATE_ROWS, 1))

    kern = functools.partial(_mlstm_core_kernel, dqk=dqk, dv=dv)
    return pl.pallas_call(
        kern,
        grid=(bsz, t // chunk),
        in_specs=[_row_tile(chunk, d), _row_tile(chunk, qk_w), _row_tile(chunk, v_w),
                  _row_tile(chunk, v_w),
                  pl.BlockSpec((None, FGATE_ROWS, chunk), lambda b, i: (b, 0, i)),
                  _resident((1, v_w)), _resident((v_w, d))],
        out_specs=_row_tile(chunk, d),
        out_shape=jax.ShapeDtypeStruct(x.shape, F32),
        scratch_shapes=[pltpu.VMEM((M_HEADS, dqk, dv), F32),
                        pltpu.VMEM((M_HEADS, 1, dqk), F32),
                        pltpu.VMEM((M_HEADS, 1, 128), F32)],
        compiler_params=_params(),
        name="mlstm_core",
    )(x, qk, v, o, gt, head_norm.reshape(1, -1), w_out.astype(BF16))


def kernel(x, norm_mix, norm_ffn, norm_final, m_w_in, m_conv_w, m_conv_b, m_b_gates, m_head_norm, m_w_out, c_w_in, c_b_in, c_dw_w, c_dw_b, c_ln_g, c_ln_b, c_w_out, c_b_out, f_w_up, f_conv_w, f_conv_b, f_w_down):
    depth = norm_mix.shape[0]
    n_mixers = 2
    for i in range(depth):
        j = i // n_mixers
        if i % n_mixers == 0:
            x = _mlstm_layer(x, norm_mix[i], m_w_in[j], m_conv_w[j], m_conv_b[j], m_b_gates[j],
                             m_head_norm[j], m_w_out[j])
        else:
            x = _conformer(x, norm_mix[i], c_w_in[j], c_b_in[j], c_dw_w[j], c_dw_b[j],
                           c_ln_g[j], c_ln_b[j], c_w_out[j], c_b_out[j])
        x = _conv_ffn(x, norm_ffn[i], f_w_up[i], f_conv_w[i], f_conv_b[i], f_w_down[i],
                      norm_final, final_norm=(i == depth - 1))
    return x
```

```python
import functools

import jax
import jax.numpy as jnp
from jax import lax
from jax.experimental import pallas as pl
from jax.experimental.pallas import tpu as pltpu

EPS = 1e-6
M_HEADS = 4
FGATE_ROWS = 8
SUBLANES = 8
LANES = 128
VMEM_LIMIT_BYTES = 56 * 1024 * 1024

F32 = jnp.float32
BF16 = jnp.bfloat16


def _rms(x, g):
    return x * lax.rsqrt(jnp.mean(x * x, axis=-1, keepdims=True) + EPS) * g


def _resident(shape):
    return pl.BlockSpec(shape, lambda b, i: (0,) * len(shape), pipeline_mode=pl.Buffered(1))


def _row_tile(tm, width):
    return pl.BlockSpec((None, tm, width), lambda b, i: (b, i, 0))


def _params():
    return pltpu.CompilerParams(dimension_semantics=("arbitrary", "arbitrary"),
                                vmem_limit_bytes=VMEM_LIMIT_BYTES)


def _slabs(tm, width):
    return pltpu.VMEM((width // LANES, tm, LANES), F32)


def _permute_rows(slab_ref, val):
    tm = val.shape[0]
    a = tm // SUBLANES
    for s in range(slab_ref.shape[0]):
        for r in range(SUBLANES):
            slab_ref[s, pl.ds(r, a, stride=SUBLANES), :] = (
                val[r * a:(r + 1) * a, s * LANES:(s + 1) * LANES])


def _load_slabs(slab_ref):
    return jnp.concatenate([slab_ref[s] for s in range(slab_ref.shape[0])], axis=1)


def _store_slabs(slab_ref, val):
    for s in range(slab_ref.shape[0]):
        slab_ref[s] = val[:, s * LANES:(s + 1) * LANES]


def _time_rows(slab_ref, r):
    a = slab_ref.shape[1] // SUBLANES
    return jnp.concatenate([slab_ref[s, pl.ds(r, a, stride=SUBLANES), :]
                            for s in range(slab_ref.shape[0])], axis=1)


def _wrap_groups(u, halo_ref, cols, taps):
    tm, n = u.shape
    nh = (taps - 1) * SUBLANES
    sub = lax.broadcasted_iota(jnp.int32, (SUBLANES, n), 0)
    groups = []
    for g in range(taps - 1):
        rows = slice(tm - nh + g * SUBLANES, tm - nh + (g + 1) * SUBLANES)
        prev = halo_ref[g * SUBLANES:(g + 1) * SUBLANES, cols]
        groups.append(pltpu.roll(jnp.where(sub == SUBLANES - 1, prev, u[rows, :]), 1, axis=0))
    halo_ref[:, cols] = u[tm - nh:, :]
    return groups


def _causal_taps(u, halo_ref, cols, w):
    tm = u.shape[0]
    k = w.shape[0]
    ext = jnp.concatenate(_wrap_groups(u, halo_ref, cols, k) + [u], axis=0)
    y = w[k - 1:k, :] * u
    for j in range(k - 1):
        y = y + w[j:j + 1, :] * ext[j * SUBLANES:j * SUBLANES + tm, :]
    return y


def _ffn_kernel(x_ref, g_ref, wup_ref, cw_ref, cb_ref, wdn_ref, gfin_ref, o_ref,
                halo_ref, hn_ref, acc_ref, *, d_ff, tf, final_norm):
    @pl.when(pl.program_id(1) == 0)
    def _():
        halo_ref[...] = jnp.zeros_like(halo_ref)

    _permute_rows(hn_ref, _rms(x_ref[...], g_ref[...]))
    hn = _load_slabs(hn_ref).astype(BF16)

    def up(c):
        return [jnp.dot(hn, wup_ref[:, half * d_ff + c * tf:half * d_ff + (c + 1) * tf],
                        preferred_element_type=F32) for half in range(2)]

    n_chunks = d_ff // tf
    us_next = up(0)
    for c in range(n_chunks):
        us, us_next = us_next, (up(c + 1) if c + 1 < n_chunks else None)
        ys = []
        for half in range(2):
            cols = slice(half * d_ff + c * tf, half * d_ff + (c + 1) * tf)
            ys.append(_causal_taps(us[half], halo_ref, cols, cw_ref[:, cols]) + cb_ref[:, cols])
        act = (ys[0] * jax.nn.sigmoid(ys[0]) * ys[1]).astype(BF16)
        contrib = jnp.dot(act, wdn_ref[c * tf:(c + 1) * tf, :], preferred_element_type=F32)
        for s in range(acc_ref.shape[0]):
            piece = contrib[:, s * LANES:(s + 1) * LANES]
            if c == 0:
                acc_ref[s] = piece
            else:
                acc_ref[s] += piece
    a = x_ref.shape[0] // SUBLANES
    for r in range(SUBLANES):
        rows = slice(r * a, (r + 1) * a)
        out = x_ref[rows, :] + _time_rows(acc_ref, r)
        if final_norm:
            out = _rms(out, gfin_ref[...])
        o_ref[rows, :] = out


def _conv_ffn(x, g, w_up, conv_w, conv_b, w_down, g_final, *, final_norm, tm=512, tf=256):
    bsz, t, d = x.shape
    d_ff = w_down.shape[0]
    taps = conv_w.shape[0]
    kern = functools.partial(_ffn_kernel, d_ff=d_ff, tf=tf, final_norm=final_norm)
    return pl.pallas_call(
        kern,
        grid=(bsz, t // tm),
        in_specs=[_row_tile(tm, d), _resident((1, d)), _resident((d, 2 * d_ff)),
                  _resident(conv_w.shape), _resident((1, 2 * d_ff)), _resident((d_ff, d)),
                  _resident((1, d))],
        out_specs=_row_tile(tm, d),
        out_shape=jax.ShapeDtypeStruct(x.shape, F32),
        scratch_shapes=[pltpu.VMEM(((taps - 1) * SUBLANES, 2 * d_ff), F32),
                        _slabs(tm, d), _slabs(tm, d)],
        compiler_params=_params(),
        name="conv_ffn",
    )(x, g.reshape(1, d), w_up.astype(BF16), conv_w, conv_b.reshape(1, -1),
      w_down.astype(BF16), g_final.reshape(1, d))


def _conformer_kernel(x_ref, g_ref, win_ref, bin_ref, dw_ref, dwb_ref, lng_ref, lnb_ref,
                      wout_ref, bout_ref, o_ref, halo_ref, slab_ref, ext_ref, wb_ref, y_ref,
                      z_ref, *, block_rows, norm_rows):
    taps = dw_ref.shape[0]
    tm, d = x_ref.shape
    nh = (taps - 1) * SUBLANES

    @pl.when(pl.program_id(1) == 0)
    def _():
        halo_ref[...] = jnp.zeros_like(halo_ref)
        for j in range(taps):
            wb_ref[j] = jnp.broadcast_to(dw_ref[j:j + 1, :], (SUBLANES, d))

    _permute_rows(slab_ref, _rms(x_ref[...], g_ref[...]))
    hn = _load_slabs(slab_ref).astype(BF16)
    a1 = jnp.dot(hn, win_ref[:, 0:d], preferred_element_type=F32) + bin_ref[:, 0:d]
    a2 = jnp.dot(hn, win_ref[:, d:2 * d], preferred_element_type=F32) + bin_ref[:, d:2 * d]
    u = a1 * jax.nn.sigmoid(a2)
    wraps = _wrap_groups(u, halo_ref, slice(0, d), taps)
    for g, piece in enumerate(wraps):
        ext_ref[g * SUBLANES:(g + 1) * SUBLANES, :] = piece
    ext_ref[nh:nh + tm, :] = u

    groups = block_rows // SUBLANES
    for lb in range(d // LANES):
        lanes = slice(lb * LANES, (lb + 1) * LANES)
        ws = [wb_ref[j, :, lanes] for j in range(taps)]
        bias = jnp.broadcast_to(dwb_ref[:, lanes], (SUBLANES, LANES))

        def conv_block(i, carry, lanes=lanes, ws=ws, bias=bias):
            base = pl.multiple_of(i * block_rows, block_rows)
            accs = [bias] * groups
            for j in range(taps):
                for q in range(groups):
                    rows = pl.ds(base + (j + q) * SUBLANES, SUBLANES)
                    accs[q] = accs[q] + ws[j] * ext_ref[rows, lanes]
            y_ref[pl.ds(base, block_rows), lanes] = jnp.concatenate(accs, axis=0)
            return carry

        lax.fori_loop(0, tm // block_rows, conv_block, 0)

    def norm_block(i, carry):
        base = pl.multiple_of(i * norm_rows, norm_rows)
        y = y_ref[pl.ds(base, norm_rows), :]
        mu = jnp.mean(y, axis=-1, keepdims=True)
        yc = y - mu
        var = jnp.mean(yc * yc, axis=-1, keepdims=True)
        z = yc * lax.rsqrt(var + EPS) * lng_ref[...] + lnb_ref[...]
        z_ref[pl.ds(base, norm_rows), :] = (z * jax.nn.sigmoid(z)).astype(BF16)
        return carry

    lax.fori_loop(0, tm // norm_rows, norm_block, 0)
    res = jnp.dot(z_ref[...], wout_ref[...], preferred_element_type=F32) + bout_ref[...]
    _store_slabs(slab_ref, res)
    a = tm // SUBLANES
    for r in range(SUBLANES):
        rows = slice(r * a, (r + 1) * a)
        o_ref[rows, :] = x_ref[rows, :] + _time_rows(slab_ref, r)


def _conformer(x, g, w_in, b_in, dw_w, dw_b, ln_g, ln_b, w_out, b_out, *, tm=512,
               block_rows=128, norm_rows=128):
    bsz, t, d = x.shape
    k = dw_w.shape[0]
    nh = (k - 1) * SUBLANES
    kern = functools.partial(_conformer_kernel, block_rows=block_rows, norm_rows=norm_rows)
    vec = lambda a: a.reshape(1, -1)
    return pl.pallas_call(
        kern,
        grid=(bsz, t // tm),
        in_specs=[_row_tile(tm, d), _resident((1, d)), _resident((d, 2 * d)),
                  _resident((1, 2 * d)), _resident((k, d)), _resident((1, d)),
                  _resident((1, d)), _resident((1, d)), _resident((d, d)), _resident((1, d))],
        out_specs=_row_tile(tm, d),
        out_shape=jax.ShapeDtypeStruct(x.shape, F32),
        scratch_shapes=[pltpu.VMEM((nh, d), F32), _slabs(tm, d),
                        pltpu.VMEM((nh + tm, d), F32),
                        pltpu.VMEM((k, SUBLANES, d), F32),
                        pltpu.VMEM((tm, d), F32),
                        pltpu.VMEM((tm, d), BF16)],
        compiler_params=_params(),
        name="conformer",
    )(x, vec(g), w_in.astype(BF16), vec(b_in), dw_w, vec(dw_b), vec(ln_g), vec(ln_b),
      w_out.astype(BF16), vec(b_out))


def _mlstm_proj_kernel(x_ref, g_ref, wqk_ref, wv_ref, wo_ref, wgt_ref, cw_ref, cb_ref, bg_ref,
                       qk_ref, v_ref, o_ref, gt_ref, halo_ref, slab_ref, *, q_scale):
    @pl.when(pl.program_id(1) == 0)
    def _():
        halo_ref[...] = jnp.zeros_like(halo_ref)

    hn32 = _rms(x_ref[...], g_ref[...])
    hn = hn32.astype(BF16)
    width = wqk_ref.shape[1]
    _permute_rows(slab_ref, hn32)
    pre = jnp.dot(_load_slabs(slab_ref).astype(BF16), wqk_ref[...], preferred_element_type=F32)
    y = _causal_taps(pre, halo_ref, slice(0, width), cw_ref[...]) + cb_ref[...]
    y = y * jax.nn.sigmoid(y)
    col = lax.broadcasted_iota(jnp.int32, (1, width), 1)
    _store_slabs(slab_ref, y * jnp.where(col < width // 2, q_scale, 1.0))
    a = x_ref.shape[0] // SUBLANES
    for r in range(SUBLANES):
        qk_ref[r * a:(r + 1) * a, :] = _time_rows(slab_ref, r).astype(BF16)
    v_ref[...] = jnp.dot(hn, wv_ref[...], preferred_element_type=F32).astype(BF16)
    o_ref[...] = jnp.dot(hn, wo_ref[...], preferred_element_type=F32).astype(BF16)
    gt = lax.dot_general(wgt_ref[...], hn, (((1,), (1,)), ((), ())),
                         preferred_element_type=F32) + bg_ref[...]
    row = lax.broadcasted_iota(jnp.int32, gt.shape, 0)
    gt_ref[...] = jnp.where(row < M_HEADS, gt, jax.nn.log_sigmoid(gt))


def _mlstm_core_kernel(x_ref, qk_ref, v_ref, o_ref, gt_ref, hnorm_ref, wout_ref, out_ref,
                       ct_ref, n_ref, m_ref, *, dqk, dv):
    @pl.when(pl.program_id(1) == 0)
    def _():
        ct_ref[...] = jnp.zeros_like(ct_ref)
        n_ref[...] = jnp.zeros_like(n_ref)
        m_ref[...] = jnp.zeros_like(m_ref)

    ln = x_ref.shape[0]
    row = lax.broadcasted_iota(jnp.int32, (ln, ln), 0)
    col = lax.broadcasted_iota(jnp.int32, (ln, ln), 1)
    tri = col <= row
    eye = col == row
    heads = []
    for h in range(M_HEADS):
        q = qk_ref[:, h * dqk:(h + 1) * dqk]
        k = qk_ref[:, (M_HEADS + h) * dqk:(M_HEADS + h + 1) * dqk]
        v = v_ref[:, h * dv:(h + 1) * dv]
        li_r = gt_ref[h:h + 1, :]
        lf_r = gt_ref[M_HEADS + h:M_HEADS + h + 1, :]
        bb_c = jnp.sum(jnp.where(tri, lf_r, 0.0), axis=1, keepdims=True)
        bb_r = jnp.sum(jnp.where(eye, bb_c, 0.0), axis=0, keepdims=True)
        li_c = jnp.sum(jnp.where(eye, li_r, 0.0), axis=1, keepdims=True)
        m_st = m_ref[h][:, 0:1]
        n_st = n_ref[h]
        ct = ct_ref[h]
        dmat = jnp.where(tri, bb_c + (li_r - bb_r), -jnp.inf)
        inter = bb_c + m_st
        m_t = jnp.maximum(inter, jnp.max(dmat, axis=1, keepdims=True))
        s = lax.dot_general(q, k, (((1,), (1,)), ((), ())), preferred_element_type=F32)
        s = s * jnp.exp(dmat - m_t)
        sc = jnp.exp(inter - m_t)
        num = (jnp.dot(s.astype(BF16), v, preferred_element_type=F32)
               + sc * jnp.dot(q, ct.astype(BF16), preferred_element_type=F32))
        den = (jnp.sum(s, axis=1, keepdims=True)
               + sc * jnp.sum(q.astype(F32) * n_st, axis=1, keepdims=True))
        hh = num / jnp.maximum(jnp.abs(den), jnp.exp(-m_t))
        b_last = bb_c[ln - 1:ln, :]
        wlog = b_last - bb_c + li_c
        m_new = jnp.maximum(b_last + m_st, jnp.max(wlog, axis=0, keepdims=True))
        decay = jnp.exp(b_last + m_st - m_new)
        ws = jnp.exp(wlog - m_new)
        wv = (ws * v.astype(F32)).astype(BF16)
        ct_ref[h] = decay * ct + lax.dot_general(k, wv, (((0,), (0,)), ((), ())),
                                                 preferred_element_type=F32)
        n_ref[h] = decay * n_st + jnp.sum(ws * k.astype(F32), axis=0, keepdims=True)
        m_ref[h] = jnp.broadcast_to(m_new, m_ref.shape[1:])
        heads.append(hh * lax.rsqrt(jnp.mean(hh * hh, axis=-1, keepdims=True) + EPS))
    hcat = jnp.concatenate(heads, axis=-1) * hnorm_ref[...]
    gated = (hcat * jax.nn.sigmoid(o_ref[...].astype(F32))).astype(BF16)
    out_ref[...] = x_ref[...] + jnp.dot(gated, wout_ref[...], preferred_element_type=F32)


def _mlstm_layer(x, g, w_in, conv_w, conv_b, b_gates, head_norm, w_out, *, tm=512, chunk=256):
    bsz, t, d = x.shape
    taps, qk_w = conv_w.shape
    v_w = head_norm.shape[0]
    dqk = qk_w // (2 * M_HEADS)
    dv = v_w // M_HEADS
    w_in = w_in.astype(BF16)
    w_qk = w_in[:, :qk_w]
    w_v = w_in[:, qk_w:qk_w + v_w]
    w_o = w_in[:, qk_w + v_w:qk_w + 2 * v_w]
    w_gt = w_in[:, qk_w + 2 * v_w:].T
    kern = functools.partial(_mlstm_proj_kernel, q_scale=dqk ** -0.5)
    qk, v, o, gt = pl.pallas_call(
        kern,
        grid=(bsz, t // tm),
        in_specs=[_row_tile(tm, d), _resident((1, d)), _resident((d, qk_w)),
                  _resident((d, v_w)), _resident((d, v_w)), _resident((FGATE_ROWS, d)),
                  _resident(conv_w.shape), _resident((1, qk_w)), _resident((FGATE_ROWS, 1))],
        out_specs=[_row_tile(tm, qk_w), _row_tile(tm, v_w), _row_tile(tm, v_w),
                   pl.BlockSpec((None, FGATE_ROWS, tm), lambda b, i: (b, 0, i))],
        out_shape=[jax.ShapeDtypeStruct((bsz, t, qk_w), BF16),
                   jax.ShapeDtypeStruct((bsz, t, v_w), BF16),
                   jax.ShapeDtypeStruct((bsz, t, v_w), BF16),
                   jax.ShapeDtypeStruct((bsz, FGATE_ROWS, t), F32)],
        scratch_shapes=[pltpu.VMEM(((taps - 1) * SUBLANES, qk_w), F32), _slabs(tm, d)],
        compiler_params=_params(),
        name="mlstm_proj",
    )(x, g.reshape(1, d), w_qk, w_v, w_o, w_gt, conv_w, conv_b.reshape(1, -1),
      b_gates.reshape(FGATE_ROWS, 1))

    kern = functools.partial(_mlstm_core_kernel, dqk=dqk, dv=dv)
    return pl.pallas_call(
        kern,
        grid=(bsz, t // chunk),
        in_specs=[_row_tile(chunk, d), _row_tile(chunk, qk_w), _row_tile(chunk, v_w),
                  _row_tile(chunk, v_w),
                  pl.BlockSpec((None, FGATE_ROWS, chunk), lambda b, i: (b, 0, i)),
                  _resident((1, v_w)), _resident((v_w, d))],
        out_specs=_row_tile(chunk, d),
        out_shape=jax.ShapeDtypeStruct(x.shape, F32),
        scratch_shapes=[pltpu.VMEM((M_HEADS, dqk, dv), F32),
                        pltpu.VMEM((M_HEADS, 1, dqk), F32),
                        pltpu.VMEM((M_HEADS, 1, LANES), F32)],
        compiler_params=_params(),
        name="mlstm_core",
    )(x, qk, v, o, gt, head_norm.reshape(1, -1), w_out.astype(BF16))


def kernel(x, norm_mix, norm_ffn, norm_final, m_w_in, m_conv_w, m_conv_b, m_b_gates, m_head_norm, m_w_out, c_w_in, c_b_in, c_dw_w, c_dw_b, c_ln_g, c_ln_b, c_w_out, c_b_out, f_w_up, f_conv_w, f_conv_b, f_w_down):
    depth = norm_mix.shape[0]
    n_mixers = 2
    for i in range(depth):
        j = i // n_mixers
        if i % n_mixers == 0:
            x = _mlstm_layer(x, norm_mix[i], m_w_in[j], m_conv_w[j], m_conv_b[j], m_b_gates[j],
                             m_head_norm[j], m_w_out[j])
        else:
            x = _conformer(x, norm_mix[i], c_w_in[j], c_b_in[j], c_dw_w[j], c_dw_b[j],
                           c_ln_g[j], c_ln_b[j], c_w_out[j], c_b_out[j])
        x = _conv_ffn(x, norm_ffn[i], f_w_up[i], f_conv_w[i], f_conv_b[i], f_w_down[i],
                      norm_final, final_norm=(i == depth - 1))
    return x
```

```python
import functools

import jax
import jax.numpy as jnp
from jax import lax
from jax.experimental import pallas as pl
from jax.experimental.pallas import tpu as pltpu

EPS = 1e-6
M_HEADS = 4
FGATE_ROWS = 8
SUBLANES = 8
LANES = 128
VMEM_LIMIT_BYTES = 56 * 1024 * 1024

F32 = jnp.float32
BF16 = jnp.bfloat16


def _rms(x, g):
    return x * lax.rsqrt(jnp.mean(x * x, axis=-1, keepdims=True) + EPS) * g


def _resident(shape):
    return pl.BlockSpec(shape, lambda b, i: (0,) * len(shape), pipeline_mode=pl.Buffered(1))


def _row_tile(tm, width):
    return pl.BlockSpec((None, tm, width), lambda b, i: (b, i, 0))


def _params():
    return pltpu.CompilerParams(dimension_semantics=("arbitrary", "arbitrary"),
                                vmem_limit_bytes=VMEM_LIMIT_BYTES)


def _slabs(tm, width):
    return pltpu.VMEM((width // LANES, tm, LANES), F32)


def _permute_rows(slab_ref, val):
    tm = val.shape[0]
    a = tm // SUBLANES
    for s in range(slab_ref.shape[0]):
        for r in range(SUBLANES):
            slab_ref[s, pl.ds(r, a, stride=SUBLANES), :] = (
                val[r * a:(r + 1) * a, s * LANES:(s + 1) * LANES])


def _load_slabs(slab_ref):
    return jnp.concatenate([slab_ref[s] for s in range(slab_ref.shape[0])], axis=1)


def _store_slabs(slab_ref, val):
    for s in range(slab_ref.shape[0]):
        slab_ref[s] = val[:, s * LANES:(s + 1) * LANES]


def _time_rows(slab_ref, r):
    a = slab_ref.shape[1] // SUBLANES
    return jnp.concatenate([slab_ref[s, pl.ds(r, a, stride=SUBLANES), :]
                            for s in range(slab_ref.shape[0])], axis=1)


def _wrap_groups(u, halo_ref, cols, taps):
    tm, n = u.shape
    nh = (taps - 1) * SUBLANES
    sub = lax.broadcasted_iota(jnp.int32, (SUBLANES, n), 0)
    groups = []
    for g in range(taps - 1):
        rows = slice(tm - nh + g * SUBLANES, tm - nh + (g + 1) * SUBLANES)
        prev = halo_ref[g * SUBLANES:(g + 1) * SUBLANES, cols]
        groups.append(pltpu.roll(jnp.where(sub == SUBLANES - 1, prev, u[rows, :]), 1, axis=0))
    halo_ref[:, cols] = u[tm - nh:, :]
    return groups


def _causal_taps(u, halo_ref, cols, w):
    tm = u.shape[0]
    k = w.shape[0]
    ext = jnp.concatenate(_wrap_groups(u, halo_ref, cols, k) + [u], axis=0)
    y = w[k - 1:k, :] * u
    for j in range(k - 1):
        y = y + w[j:j + 1, :] * ext[j * SUBLANES:j * SUBLANES + tm, :]
    return y


def _ffn_kernel(x_ref, g_ref, wup_ref, cw_ref, cb_ref, wdn_ref, gfin_ref, o_ref,
                halo_ref, hn_ref, acc_ref, *, d_ff, tf, final_norm):
    @pl.when(pl.program_id(1) == 0)
    def _():
        halo_ref[...] = jnp.zeros_like(halo_ref)

    _permute_rows(hn_ref, _rms(x_ref[...], g_ref[...]))
    hn = _load_slabs(hn_ref).astype(BF16)

    def up(c):
        return [jnp.dot(hn, wup_ref[:, half * d_ff + c * tf:half * d_ff + (c + 1) * tf],
                        preferred_element_type=F32) for half in range(2)]

    n_chunks = d_ff // tf
    us_next = up(0)
    for c in range(n_chunks):
        us, us_next = us_next, (up(c + 1) if c + 1 < n_chunks else None)
        ys = []
        for half in range(2):
            cols = slice(half * d_ff + c * tf, half * d_ff + (c + 1) * tf)
            ys.append(_causal_taps(us[half], halo_ref, cols, cw_ref[:, cols]) + cb_ref[:, cols])
        act = (ys[0] * jax.nn.sigmoid(ys[0]) * ys[1]).astype(BF16)
        contrib = jnp.dot(act, wdn_ref[c * tf:(c + 1) * tf, :], preferred_element_type=F32)
        for s in range(acc_ref.shape[0]):
            piece = contrib[:, s * LANES:(s + 1) * LANES]
            if c == 0:
                acc_ref[s] = piece
            else:
                acc_ref[s] += piece
    a = x_ref.shape[0] // SUBLANES
    for r in range(SUBLANES):
        rows = slice(r * a, (r + 1) * a)
        out = x_ref[rows, :] + _time_rows(acc_ref, r)
        if final_norm:
            out = _rms(out, gfin_ref[...])
        o_ref[rows, :] = out


def _conv_ffn(x, g, w_up, conv_w, conv_b, w_down, g_final, *, final_norm, tm=512, tf=256):
    bsz, t, d = x.shape
    d_ff = w_down.shape[0]
    taps = conv_w.shape[0]
    kern = functools.partial(_ffn_kernel, d_ff=d_ff, tf=tf, final_norm=final_norm)
    return pl.pallas_call(
        kern,
        grid=(bsz, t // tm),
        in_specs=[_row_tile(tm, d), _resident((1, d)), _resident((d, 2 * d_ff)),
                  _resident(conv_w.shape), _resident((1, 2 * d_ff)), _resident((d_ff, d)),
                  _resident((1, d))],
        out_specs=_row_tile(tm, d),
        out_shape=jax.ShapeDtypeStruct(x.shape, F32),
        scratch_shapes=[pltpu.VMEM(((taps - 1) * SUBLANES, 2 * d_ff), F32),
                        _slabs(tm, d), _slabs(tm, d)],
        compiler_params=_params(),
        name="conv_ffn",
    )(x, g.reshape(1, d), w_up.astype(BF16), conv_w, conv_b.reshape(1, -1),
      w_down.astype(BF16), g_final.reshape(1, d))


PACKED = 16
PITCH = 24


def _permute_rows_packed(slab_ref, val):
    a = val.shape[0] // PACKED
    for s in range(slab_ref.shape[0]):
        for r in range(PACKED):
            slab_ref[s, pl.ds(r, a, stride=PITCH), :] = (
                val[r * a:(r + 1) * a, s * LANES:(s + 1) * LANES])


def _load_groups(slab_ref):
    a = slab_ref.shape[1] // PITCH
    return jnp.concatenate(
        [jnp.concatenate([slab_ref[s, g * PITCH:g * PITCH + PACKED, :]
                          for s in range(slab_ref.shape[0])], axis=1) for g in range(a)], axis=0)


def _store_groups(slab_ref, val):
    a = slab_ref.shape[1] // PITCH
    for s in range(slab_ref.shape[0]):
        for g in range(a):
            slab_ref[s, g * PITCH:g * PITCH + PACKED, :] = (
                val[g * PACKED:(g + 1) * PACKED, s * LANES:(s + 1) * LANES])


def _time_rows_packed(slab_ref, r):
    a = slab_ref.shape[1] // PITCH
    return jnp.concatenate([slab_ref[s, pl.ds(r, a, stride=PITCH), :]
                            for s in range(slab_ref.shape[0])], axis=1)


def _conformer_kernel(x_ref, g_ref, win_ref, bin_ref, dw_ref, dwb_ref, lng_ref, lnb_ref,
                      wout_ref, bout_ref, o_ref, halo_ref, slab_ref, ext_ref, wb_ref, y_ref,
                      z_ref, *, tn, block_rows, norm_rows):
    taps = dw_ref.shape[0]
    tm, d = x_ref.shape
    nh = (taps - 1) * PACKED
    a = tm // PACKED

    @pl.when(pl.program_id(1) == 0)
    def _():
        halo_ref[...] = jnp.zeros_like(halo_ref)
        for j in range(taps):
            wb_ref[j] = jnp.broadcast_to(dw_ref[j:j + 1, :], (PACKED, d)).astype(BF16)

    _permute_rows_packed(slab_ref, _rms(x_ref[...], g_ref[...]))
    hn = _load_groups(slab_ref).astype(BF16)

    def glu_in(c):
        return [jnp.dot(hn, win_ref[:, half * d + c * tn:half * d + (c + 1) * tn],
                        preferred_element_type=F32) for half in range(2)]

    last_row = lax.broadcasted_iota(jnp.int32, (PACKED, tn), 0) == PACKED - 1
    n_chunks = d // tn
    a_next = glu_in(0)
    for c in range(n_chunks):
        (a1, a2), a_next = a_next, (glu_in(c + 1) if c + 1 < n_chunks else None)
        cols = slice(c * tn, (c + 1) * tn)
        u = ((a1 + bin_ref[:, cols])
             * jax.nn.sigmoid(a2 + bin_ref[:, d + c * tn:d + (c + 1) * tn]))
        for g in range(taps - 1):
            rows = slice(tm - nh + g * PACKED, tm - nh + (g + 1) * PACKED)
            prev = halo_ref[g * PACKED:(g + 1) * PACKED, cols]
            ext_ref[g * PACKED:(g + 1) * PACKED, cols] = pltpu.roll(
                jnp.where(last_row, prev, u[rows, :]), 1, axis=0).astype(BF16)
        halo_ref[:, cols] = u[tm - nh:, :]
        ext_ref[nh:nh + tm, cols] = u.astype(BF16)

    groups = block_rows // PACKED

    def conv_block(i, carry):
        base = pl.multiple_of(i * block_rows, block_rows)
        for lb in range(d // LANES):
            lanes = slice(lb * LANES, (lb + 1) * LANES)
            accs = [None] * groups
            for e in range(taps + groups - 1):
                xe = ext_ref[pl.ds(base + e * PACKED, PACKED), lanes].astype(F32)
                for q in range(groups):
                    j = e - q
                    if 0 <= j < taps:
                        prod = xe * wb_ref[j, :, lanes].astype(F32)
                        accs[q] = prod if accs[q] is None else accs[q] + prod
            y_ref[pl.ds(base, block_rows), lanes] = (jnp.concatenate(accs, axis=0)
                                                     + dwb_ref[:, lanes])
        return carry

    lax.fori_loop(0, tm // block_rows, conv_block, 0)

    def norm_block(i, carry):
        base = pl.multiple_of(i * norm_rows, norm_rows)
        y = y_ref[pl.ds(base, norm_rows), :]
        mu = jnp.mean(y, axis=-1, keepdims=True)
        yc = y - mu
        var = jnp.mean(yc * yc, axis=-1, keepdims=True)
        z = yc * lax.rsqrt(var + EPS) * lng_ref[...] + lnb_ref[...]
        z_ref[pl.ds(base, norm_rows), :] = (z * jax.nn.sigmoid(z)).astype(BF16)
        return carry

    lax.fori_loop(0, tm // norm_rows, norm_block, 0)
    res = jnp.dot(z_ref[...], wout_ref[...], preferred_element_type=F32) + bout_ref[...]
    _store_groups(slab_ref, res)
    for r in range(PACKED):
        rows = slice(r * a, (r + 1) * a)
        o_ref[rows, :] = x_ref[rows, :] + _time_rows_packed(slab_ref, r)


def _conformer(x, g, w_in, b_in, dw_w, dw_b, ln_g, ln_b, w_out, b_out, *, tm=512,
               tn=256, block_rows=64, norm_rows=128):
    bsz, t, d = x.shape
    k = dw_w.shape[0]
    nh = (k - 1) * PACKED
    kern = functools.partial(_conformer_kernel, tn=tn, block_rows=block_rows,
                             norm_rows=norm_rows)
    vec = lambda a: a.reshape(1, -1)
    return pl.pallas_call(
        kern,
        grid=(bsz, t // tm),
        in_specs=[_row_tile(tm, d), _resident((1, d)), _resident((d, 2 * d)),
                  _resident((1, 2 * d)), _resident((k, d)), _resident((1, d)),
                  _resident((1, d)), _resident((1, d)), _resident((d, d)), _resident((1, d))],
        out_specs=_row_tile(tm, d),
        out_shape=jax.ShapeDtypeStruct(x.shape, F32),
        scratch_shapes=[pltpu.VMEM((nh, d), F32),
                        pltpu.VMEM((d // LANES, tm // PACKED * PITCH, LANES), F32),
                        pltpu.VMEM((nh + tm, d), BF16),
                        pltpu.VMEM((k, PACKED, d), BF16),
                        pltpu.VMEM((tm, d), F32),
                        pltpu.VMEM((tm, d), BF16)],
        compiler_params=_params(),
        name="conformer",
    )(x, vec(g), w_in.astype(BF16), vec(b_in), dw_w, vec(dw_b), vec(ln_g), vec(ln_b),
      w_out.astype(BF16), vec(b_out))


def _mlstm_proj_kernel(x_ref, g_ref, wqk_ref, wv_ref, wo_ref, wgt_ref, cw_ref, cb_ref, bg_ref,
                       qk_ref, v_ref, o_ref, gt_ref, halo_ref, slab_ref, yslab_ref,
                       *, q_scale, tn):
    @pl.when(pl.program_id(1) == 0)
    def _():
        halo_ref[...] = jnp.zeros_like(halo_ref)

    hn32 = _rms(x_ref[...], g_ref[...])
    hn = hn32.astype(BF16)
    width = wqk_ref.shape[1]
    _permute_rows(slab_ref, hn32)
    hn_perm = _load_slabs(slab_ref).astype(BF16)
    jobs = []
    for c in range(width // tn):
        jobs += [(hn_perm, wqk_ref, None, c), (hn, wv_ref, v_ref, c), (hn, wo_ref, o_ref, c)]
    a = x_ref.shape[0] // SUBLANES
    slabs_per_chunk = tn // LANES

    def project(job):
        lhs, w_ref, _, c = job
        return jnp.dot(lhs, w_ref[:, c * tn:(c + 1) * tn], preferred_element_type=F32)

    nxt = project(jobs[0])
    for idx, (_, _, dst_ref, c) in enumerate(jobs):
        cur, nxt = nxt, (project(jobs[idx + 1]) if idx + 1 < len(jobs) else None)
        cols = slice(c * tn, (c + 1) * tn)
        if dst_ref is None:
            y = _causal_taps(cur, halo_ref, cols, cw_ref[:, cols]) + cb_ref[:, cols]
            y = y * jax.nn.sigmoid(y)
            if (c + 1) * tn <= width // 2:
                y = y * q_scale
            chunk_slabs = yslab_ref.at[c * slabs_per_chunk:(c + 1) * slabs_per_chunk]
            _store_slabs(chunk_slabs, y)
            for r in range(SUBLANES):
                qk_ref[r * a:(r + 1) * a, cols] = _time_rows(chunk_slabs, r).astype(BF16)
        else:
            dst_ref[:, cols] = cur.astype(BF16)
    gt = lax.dot_general(wgt_ref[...], hn, (((1,), (1,)), ((), ())),
                         preferred_element_type=F32) + bg_ref[...]
    row = lax.broadcasted_iota(jnp.int32, gt.shape, 0)
    gt_ref[...] = jnp.where(row < M_HEADS, gt, jax.nn.log_sigmoid(gt))


def _mlstm_core_kernel(x_ref, qk_ref, v_ref, o_ref, gt_ref, hnorm_ref, wout_ref, out_ref,
                       ct_ref, n_ref, m_ref, *, dqk, dv):
    @pl.when(pl.program_id(1) == 0)
    def _():
        ct_ref[...] = jnp.zeros_like(ct_ref)
        n_ref[...] = jnp.zeros_like(n_ref)
        m_ref[...] = jnp.zeros_like(m_ref)

    ln = x_ref.shape[0]
    row = lax.broadcasted_iota(jnp.int32, (ln, ln), 0)
    col = lax.broadcasted_iota(jnp.int32, (ln, ln), 1)
    tri = col <= row
    eye = col == row
    heads = []
    for h in range(M_HEADS):
        q = qk_ref[:, h * dqk:(h + 1) * dqk]
        k = qk_ref[:, (M_HEADS + h) * dqk:(M_HEADS + h + 1) * dqk]
        v = v_ref[:, h * dv:(h + 1) * dv]
        li_r = gt_ref[h:h + 1, :]
        lf_r = gt_ref[M_HEADS + h:M_HEADS + h + 1, :]
        bb_c = jnp.sum(jnp.where(tri, lf_r, 0.0), axis=1, keepdims=True)
        bb_r = jnp.sum(jnp.where(eye, bb_c, 0.0), axis=0, keepdims=True)
        li_c = jnp.sum(jnp.where(eye, li_r, 0.0), axis=1, keepdims=True)
        m_st = m_ref[h][:, 0:1]
        n_st = n_ref[h]
        ct = ct_ref[h]
        dmat = jnp.where(tri, bb_c + (li_r - bb_r), -jnp.inf)
        inter = bb_c + m_st
        m_t = jnp.maximum(inter, jnp.max(dmat, axis=1, keepdims=True))
        s = lax.dot_general(q, k, (((1,), (1,)), ((), ())), preferred_element_type=F32)
        s = s * jnp.exp(dmat - m_t)
        sc = jnp.exp(inter - m_t)
        num = (jnp.dot(s.astype(BF16), v, preferred_element_type=F32)
               + sc * jnp.dot(q, ct.astype(BF16), preferred_element_type=F32))
        den = (jnp.sum(s, axis=1, keepdims=True)
               + sc * jnp.sum(q.astype(F32) * n_st, axis=1, keepdims=True))
        hh = num / jnp.maximum(jnp.abs(den), jnp.exp(-m_t))
        b_last = bb_c[ln - 1:ln, :]
        wlog = b_last - bb_c + li_c
        m_new = jnp.maximum(b_last + m_st, jnp.max(wlog, axis=0, keepdims=True))
        decay = jnp.exp(b_last + m_st - m_new)
        ws = jnp.exp(wlog - m_new)
        wv = (ws * v.astype(F32)).astype(BF16)
        ct_ref[h] = decay * ct + lax.dot_general(k, wv, (((0,), (0,)), ((), ())),
                                                 preferred_element_type=F32)
        n_ref[h] = decay * n_st + jnp.sum(ws * k.astype(F32), axis=0, keepdims=True)
        m_ref[h] = jnp.broadcast_to(m_new, m_ref.shape[1:])
        heads.append(hh * lax.rsqrt(jnp.mean(hh * hh, axis=-1, keepdims=True) + EPS))
    hcat = jnp.concatenate(heads, axis=-1) * hnorm_ref[...]
    gated = (hcat * jax.nn.sigmoid(o_ref[...].astype(F32))).astype(BF16)
    out_ref[...] = x_ref[...] + jnp.dot(gated, wout_ref[...], preferred_element_type=F32)


def _mlstm_layer(x, g, w_in, conv_w, conv_b, b_gates, head_norm, w_out, *, tm=512, tn=256,
                 chunk=256):
    bsz, t, d = x.shape
    taps, qk_w = conv_w.shape
    v_w = head_norm.shape[0]
    dqk = qk_w // (2 * M_HEADS)
    dv = v_w // M_HEADS
    w_in = w_in.astype(BF16)
    w_qk = w_in[:, :qk_w]
    w_v = w_in[:, qk_w:qk_w + v_w]
    w_o = w_in[:, qk_w + v_w:qk_w + 2 * v_w]
    w_gt = w_in[:, qk_w + 2 * v_w:].T
    kern = functools.partial(_mlstm_proj_kernel, q_scale=dqk ** -0.5, tn=tn)
    qk, v, o, gt = pl.pallas_call(
        kern,
        grid=(bsz, t // tm),
        in_specs=[_row_tile(tm, d), _resident((1, d)), _resident((d, qk_w)),
                  _resident((d, v_w)), _resident((d, v_w)), _resident((FGATE_ROWS, d)),
                  _resident(conv_w.shape), _resident((1, qk_w)), _resident((FGATE_ROWS, 1))],
        out_specs=[_row_tile(tm, qk_w), _row_tile(tm, v_w), _row_tile(tm, v_w),
                   pl.BlockSpec((None, FGATE_ROWS, tm), lambda b, i: (b, 0, i))],
        out_shape=[jax.ShapeDtypeStruct((bsz, t, qk_w), BF16),
                   jax.ShapeDtypeStruct((bsz, t, v_w), BF16),
                   jax.ShapeDtypeStruct((bsz, t, v_w), BF16),
                   jax.ShapeDtypeStruct((bsz, FGATE_ROWS, t), F32)],
        scratch_shapes=[pltpu.VMEM(((taps - 1) * SUBLANES, qk_w), F32), _slabs(tm, d),
                        _slabs(tm, qk_w)],
        compiler_params=_params(),
        name="mlstm_proj",
    )(x, g.reshape(1, d), w_qk, w_v, w_o, w_gt, conv_w, conv_b.reshape(1, -1),
      b_gates.reshape(FGATE_ROWS, 1))

    kern = functools.partial(_mlstm_core_kernel, dqk=dqk, dv=dv)
    return pl.pallas_call(
        kern,
        grid=(bsz, t // chunk),
        in_specs=[_row_tile(chunk, d), _row_tile(chunk, qk_w), _row_tile(chunk, v_w),
                  _row_tile(chunk, v_w),
                  pl.BlockSpec((None, FGATE_ROWS, chunk), lambda b, i: (b, 0, i)),
                  _resident((1, v_w)), _resident((v_w, d))],
        out_specs=_row_tile(chunk, d),
        out_shape=jax.ShapeDtypeStruct(x.shape, F32),
        scratch_shapes=[pltpu.VMEM((M_HEADS, dqk, dv), F32),
                        pltpu.VMEM((M_HEADS, 1, dqk), F32),
                        pltpu.VMEM((M_HEADS, 1, LANES), F32)],
        compiler_params=_params(),
        name="mlstm_core",
    )(x, qk, v, o, gt, head_norm.reshape(1, -1), w_out.astype(BF16))


def kernel(x, norm_mix, norm_ffn, norm_final, m_w_in, m_conv_w, m_conv_b, m_b_gates, m_head_norm, m_w_out, c_w_in, c_b_in, c_dw_w, c_dw_b, c_ln_g, c_ln_b, c_w_out, c_b_out, f_w_up, f_conv_w, f_conv_b, f_w_down):
    depth = norm_mix.shape[0]
    n_mixers = 2
    for i in range(depth):
        j = i // n_mixers
        if i % n_mixers == 0:
            x = _mlstm_layer(x, norm_mix[i], m_w_in[j], m_conv_w[j], m_conv_b[j], m_b_gates[j],
                             m_head_norm[j], m_w_out[j])
        else:
            x = _conformer(x, norm_mix[i], c_w_in[j], c_b_in[j], c_dw_w[j], c_dw_b[j],
                           c_ln_g[j], c_ln_b[j], c_w_out[j], c_b_out[j])
        x = _conv_ffn(x, norm_ffn[i], f_w_up[i], f_conv_w[i], f_conv_b[i], f_w_down[i],
                      norm_final, final_norm=(i == depth - 1))
    return x
```

```python
import functools

import jax
import jax.numpy as jnp
from jax import lax
from jax.experimental import pallas as pl
from jax.experimental.pallas import tpu as pltpu

EPS = 1e-6
M_HEADS = 4
FGATE_ROWS = 8
SUBLANES = 8
LANES = 128
VMEM_LIMIT_BYTES = 56 * 1024 * 1024

F32 = jnp.float32
BF16 = jnp.bfloat16


def _rms(x, g):
    return x * lax.rsqrt(jnp.mean(x * x, axis=-1, keepdims=True) + EPS) * g


def _resident(shape):
    return pl.BlockSpec(shape, lambda b, i: (0,) * len(shape), pipeline_mode=pl.Buffered(1))


def _row_tile(tm, width):
    return pl.BlockSpec((None, tm, width), lambda b, i: (b, i, 0))


def _params():
    return pltpu.CompilerParams(dimension_semantics=("arbitrary", "arbitrary"),
                                vmem_limit_bytes=VMEM_LIMIT_BYTES)


def _slabs(tm, width):
    return pltpu.VMEM((width // LANES, tm, LANES), F32)


def _permute_rows(slab_ref, val):
    tm = val.shape[0]
    a = tm // SUBLANES
    for s in range(slab_ref.shape[0]):
        for r in range(SUBLANES):
            slab_ref[s, pl.ds(r, a, stride=SUBLANES), :] = (
                val[r * a:(r + 1) * a, s * LANES:(s + 1) * LANES])


def _load_slabs(slab_ref):
    return jnp.concatenate([slab_ref[s] for s in range(slab_ref.shape[0])], axis=1)


def _store_slabs(slab_ref, val):
    for s in range(slab_ref.shape[0]):
        slab_ref[s] = val[:, s * LANES:(s + 1) * LANES]


def _time_rows(slab_ref, r):
    a = slab_ref.shape[1] // SUBLANES
    return jnp.concatenate([slab_ref[s, pl.ds(r, a, stride=SUBLANES), :]
                            for s in range(slab_ref.shape[0])], axis=1)


def _wrap_groups(u, halo_ref, cols, taps):
    tm, n = u.shape
    nh = (taps - 1) * SUBLANES
    sub = lax.broadcasted_iota(jnp.int32, (SUBLANES, n), 0)
    groups = []
    for g in range(taps - 1):
        rows = slice(tm - nh + g * SUBLANES, tm - nh + (g + 1) * SUBLANES)
        prev = halo_ref[g * SUBLANES:(g + 1) * SUBLANES, cols]
        groups.append(pltpu.roll(jnp.where(sub == SUBLANES - 1, prev, u[rows, :]), 1, axis=0))
    halo_ref[:, cols] = u[tm - nh:, :]
    return groups


def _causal_taps(u, halo_ref, cols, w):
    tm = u.shape[0]
    k = w.shape[0]
    ext = jnp.concatenate(_wrap_groups(u, halo_ref, cols, k) + [u], axis=0)
    y = w[k - 1:k, :] * u
    for j in range(k - 1):
        y = y + w[j:j + 1, :] * ext[j * SUBLANES:j * SUBLANES + tm, :]
    return y


def _ffn_kernel(x_ref, g_ref, wup_ref, cw_ref, cb_ref, wdn_ref, gfin_ref, o_ref,
                halo_ref, hn_ref, acc_ref, *, d_ff, tf, final_norm):
    @pl.when(pl.program_id(1) == 0)
    def _():
        halo_ref[...] = jnp.zeros_like(halo_ref)

    _permute_rows(hn_ref, _rms(x_ref[...], g_ref[...]))
    hn = _load_slabs(hn_ref).astype(BF16)

    bounds = [(lo, min(lo + tf, d_ff)) for lo in range(0, d_ff, tf)]

    def up(c):
        lo, hi = bounds[c]
        return [jnp.dot(hn, wup_ref[:, half * d_ff + lo:half * d_ff + hi],
                        preferred_element_type=F32) for half in range(2)]

    n_chunks = len(bounds)
    us_next = up(0)
    for c in range(n_chunks):
        us, us_next = us_next, (up(c + 1) if c + 1 < n_chunks else None)
        lo, hi = bounds[c]
        ys = []
        for half in range(2):
            cols = slice(half * d_ff + lo, half * d_ff + hi)
            ys.append(_causal_taps(us[half], halo_ref, cols, cw_ref[:, cols]) + cb_ref[:, cols])
        act = (ys[0] * jax.nn.sigmoid(ys[0]) * ys[1]).astype(BF16)
        contrib = jnp.dot(act, wdn_ref[lo:hi, :], preferred_element_type=F32)
        for s in range(acc_ref.shape[0]):
            piece = contrib[:, s * LANES:(s + 1) * LANES]
            if c == 0:
                acc_ref[s] = piece
            else:
                acc_ref[s] += piece
    a = x_ref.shape[0] // SUBLANES
    for r in range(SUBLANES):
        rows = slice(r * a, (r + 1) * a)
        out = x_ref[rows, :] + _time_rows(acc_ref, r)
        if final_norm:
            out = _rms(out, gfin_ref[...])
        o_ref[rows, :] = out


def _conv_ffn(x, g, w_up, conv_w, conv_b, w_down, g_final, *, final_norm, tm=512, tf=768):
    bsz, t, d = x.shape
    d_ff = w_down.shape[0]
    taps = conv_w.shape[0]
    kern = functools.partial(_ffn_kernel, d_ff=d_ff, tf=tf, final_norm=final_norm)
    return pl.pallas_call(
        kern,
        grid=(bsz, t // tm),
        in_specs=[_row_tile(tm, d), _resident((1, d)), _resident((d, 2 * d_ff)),
                  _resident(conv_w.shape), _resident((1, 2 * d_ff)), _resident((d_ff, d)),
                  _resident((1, d))],
        out_specs=_row_tile(tm, d),
        out_shape=jax.ShapeDtypeStruct(x.shape, F32),
        scratch_shapes=[pltpu.VMEM(((taps - 1) * SUBLANES, 2 * d_ff), F32),
                        _slabs(tm, d), _slabs(tm, d)],
        compiler_params=_params(),
        name="conv_ffn",
    )(x, g.reshape(1, d), w_up.astype(BF16), conv_w, conv_b.reshape(1, -1),
      w_down.astype(BF16), g_final.reshape(1, d))


def _conformer_kernel(x_ref, g_ref, win_ref, bin_ref, dw_ref, dwb_ref, lng_ref, lnb_ref,
                      wout_ref, bout_ref, o_ref, halo_ref, slab_ref, ext_ref, wb_ref, y_ref,
                      z_ref, *, block_rows, norm_rows):
    taps = dw_ref.shape[0]
    tm, d = x_ref.shape
    nh = (taps - 1) * SUBLANES

    @pl.when(pl.program_id(1) == 0)
    def _():
        halo_ref[...] = jnp.zeros_like(halo_ref)
        for j in range(taps):
            wb_ref[j] = jnp.broadcast_to(dw_ref[j:j + 1, :], (SUBLANES, d))

    _permute_rows(slab_ref, _rms(x_ref[...], g_ref[...]))
    hn = _load_slabs(slab_ref).astype(BF16)
    a1 = jnp.dot(hn, win_ref[:, 0:d], preferred_element_type=F32) + bin_ref[:, 0:d]
    a2 = jnp.dot(hn, win_ref[:, d:2 * d], preferred_element_type=F32) + bin_ref[:, d:2 * d]
    u = a1 * jax.nn.sigmoid(a2)
    wraps = _wrap_groups(u, halo_ref, slice(0, d), taps)
    for g, piece in enumerate(wraps):
        ext_ref[g * SUBLANES:(g + 1) * SUBLANES, :] = piece
    ext_ref[nh:nh + tm, :] = u

    groups = block_rows // SUBLANES
    for lb in range(d // LANES):
        lanes = slice(lb * LANES, (lb + 1) * LANES)
        ws = [wb_ref[j, :, lanes] for j in range(taps)]
        bias = jnp.broadcast_to(dwb_ref[:, lanes], (SUBLANES, LANES))

        def conv_block(i, carry, lanes=lanes, ws=ws, bias=bias):
            base = pl.multiple_of(i * block_rows, block_rows)
            accs = [bias] * groups
            for j in range(taps):
                for q in range(groups):
                    rows = pl.ds(base + (j + q) * SUBLANES, SUBLANES)
                    accs[q] = accs[q] + ws[j] * ext_ref[rows, lanes]
            y_ref[pl.ds(base, block_rows), lanes] = jnp.concatenate(accs, axis=0)
            return carry

        lax.fori_loop(0, tm // block_rows, conv_block, 0)

    def norm_block(i, carry):
        base = pl.multiple_of(i * norm_rows, norm_rows)
        y = y_ref[pl.ds(base, norm_rows), :]
        mu = jnp.mean(y, axis=-1, keepdims=True)
        yc = y - mu
        var = jnp.mean(yc * yc, axis=-1, keepdims=True)
        z = yc * lax.rsqrt(var + EPS) * lng_ref[...] + lnb_ref[...]
        z_ref[pl.ds(base, norm_rows), :] = (z * jax.nn.sigmoid(z)).astype(BF16)
        return carry

    lax.fori_loop(0, tm // norm_rows, norm_block, 0)
    res = jnp.dot(z_ref[...], wout_ref[...], preferred_element_type=F32) + bout_ref[...]
    _store_slabs(slab_ref, res)
    a = tm // SUBLANES
    for r in range(SUBLANES):
        rows = slice(r * a, (r + 1) * a)
        o_ref[rows, :] = x_ref[rows, :] + _time_rows(slab_ref, r)


def _conformer(x, g, w_in, b_in, dw_w, dw_b, ln_g, ln_b, w_out, b_out, *, tm=512,
               block_rows=128, norm_rows=128):
    bsz, t, d = x.shape
    k = dw_w.shape[0]
    nh = (k - 1) * SUBLANES
    kern = functools.partial(_conformer_kernel, block_rows=block_rows, norm_rows=norm_rows)
    vec = lambda a: a.reshape(1, -1)
    return pl.pallas_call(
        kern,
        grid=(bsz, t // tm),
        in_specs=[_row_tile(tm, d), _resident((1, d)), _resident((d, 2 * d)),
                  _resident((1, 2 * d)), _resident((k, d)), _resident((1, d)),
                  _resident((1, d)), _resident((1, d)), _resident((d, d)), _resident((1, d))],
        out_specs=_row_tile(tm, d),
        out_shape=jax.ShapeDtypeStruct(x.shape, F32),
        scratch_shapes=[pltpu.VMEM((nh, d), F32), _slabs(tm, d),
                        pltpu.VMEM((nh + tm, d), F32),
                        pltpu.VMEM((k, SUBLANES, d), F32),
                        pltpu.VMEM((tm, d), F32),
                        pltpu.VMEM((tm, d), BF16)],
        compiler_params=_params(),
        name="conformer",
    )(x, vec(g), w_in.astype(BF16), vec(b_in), dw_w, vec(dw_b), vec(ln_g), vec(ln_b),
      w_out.astype(BF16), vec(b_out))


def _mlstm_proj_kernel(x_ref, g_ref, win_ref, wgt_ref, cw_ref, cb_ref, bg_ref,
                       qk_ref, v_ref, o_ref, gt_ref, halo_ref, slab_ref, yslab_ref,
                       *, q_scale, tn):
    @pl.when(pl.program_id(1) == 0)
    def _():
        halo_ref[...] = jnp.zeros_like(halo_ref)

    hn32 = _rms(x_ref[...], g_ref[...])
    hn = hn32.astype(BF16)
    width = cw_ref.shape[1]
    _permute_rows(slab_ref, hn32)
    hn_perm = _load_slabs(slab_ref).astype(BF16)
    jobs = []
    for c in range(width // tn):
        jobs += [(hn_perm, 0, None, c), (hn, width, v_ref, c), (hn, 2 * width, o_ref, c)]
    a = x_ref.shape[0] // SUBLANES
    slabs_per_chunk = tn // LANES

    def project(job):
        lhs, col0, _, c = job
        return jnp.dot(lhs, win_ref[:, col0 + c * tn:col0 + (c + 1) * tn],
                       preferred_element_type=F32)

    nxt = project(jobs[0])
    for idx, (_, _, dst_ref, c) in enumerate(jobs):
        cur, nxt = nxt, (project(jobs[idx + 1]) if idx + 1 < len(jobs) else None)
        cols = slice(c * tn, (c + 1) * tn)
        if dst_ref is None:
            y = _causal_taps(cur, halo_ref, cols, cw_ref[:, cols]) + cb_ref[:, cols]
            y = y * jax.nn.sigmoid(y)
            if (c + 1) * tn <= width // 2:
                y = y * q_scale
            chunk_slabs = yslab_ref.at[c * slabs_per_chunk:(c + 1) * slabs_per_chunk]
            _store_slabs(chunk_slabs, y)
            for r in range(SUBLANES):
                qk_ref[r * a:(r + 1) * a, cols] = _time_rows(chunk_slabs, r).astype(BF16)
        else:
            dst_ref[:, cols] = cur.astype(BF16)
    gt = lax.dot_general(wgt_ref[...], hn, (((1,), (1,)), ((), ())),
                         preferred_element_type=F32) + bg_ref[...]
    row = lax.broadcasted_iota(jnp.int32, gt.shape, 0)
    gt_ref[...] = jnp.where(row < M_HEADS, gt, jax.nn.log_sigmoid(gt))


def _mlstm_core_kernel(x_ref, qk_ref, v_ref, o_ref, gt_ref, hnorm_ref, wout_ref, out_ref,
                       ct_ref, n_ref, m_ref, *, dqk, dv, chunk):
    @pl.when(pl.program_id(1) == 0)
    def _():
        ct_ref[...] = jnp.zeros_like(ct_ref)
        n_ref[...] = jnp.zeros_like(n_ref)
        m_ref[...] = jnp.zeros_like(m_ref)

    ln = chunk
    row = lax.broadcasted_iota(jnp.int32, (ln, ln), 0)
    col = lax.broadcasted_iota(jnp.int32, (ln, ln), 1)
    tri = col <= row
    eye = col == row
    for sub in range(x_ref.shape[0] // chunk):
        rows = slice(sub * chunk, (sub + 1) * chunk)
        _mlstm_chunk(x_ref, qk_ref, v_ref, o_ref, gt_ref, hnorm_ref, wout_ref, out_ref,
                     ct_ref, n_ref, m_ref, rows, tri, eye, dqk, dv)


def _row_reduce(combine, reduce, x):
    acc = x[:, :LANES]
    for lb in range(1, x.shape[1] // LANES):
        acc = combine(acc, x[:, lb * LANES:(lb + 1) * LANES])
    return reduce(acc, axis=1, keepdims=True)


def _mlstm_chunk(x_ref, qk_ref, v_ref, o_ref, gt_ref, hnorm_ref, wout_ref, out_ref,
                 ct_ref, n_ref, m_ref, rows, tri, eye, dqk, dv):
    ln = rows.stop - rows.start
    heads = []
    for h in range(M_HEADS):
        q = qk_ref[rows, h * dqk:(h + 1) * dqk]
        k = qk_ref[rows, (M_HEADS + h) * dqk:(M_HEADS + h + 1) * dqk]
        v = v_ref[rows, h * dv:(h + 1) * dv]
        li_r = gt_ref[h:h + 1, rows]
        lf_r = gt_ref[M_HEADS + h:M_HEADS + h + 1, rows]
        bb_c = _row_reduce(jnp.add, jnp.sum, jnp.where(tri, lf_r, 0.0))
        bb_r = jnp.sum(jnp.where(eye, bb_c, 0.0), axis=0, keepdims=True)
        li_c = _row_reduce(jnp.add, jnp.sum, jnp.where(eye, li_r, 0.0))
        m_st = m_ref[h][:, 0:1]
        n_st = n_ref[h]
        ct = ct_ref[h]
        dmat = jnp.where(tri, bb_c + (li_r - bb_r), -jnp.inf)
        inter = bb_c + m_st
        m_t = jnp.maximum(inter, _row_reduce(jnp.maximum, jnp.max, dmat))
        s = lax.dot_general(q, k, (((1,), (1,)), ((), ())), preferred_element_type=F32)
        s = s * jnp.exp(dmat - m_t)
        sc = jnp.exp(inter - m_t)
        num = (jnp.dot(s.astype(BF16), v, preferred_element_type=F32)
               + sc * jnp.dot(q, ct.astype(BF16), preferred_element_type=F32))
        den = (_row_reduce(jnp.add, jnp.sum, s)
               + sc * jnp.sum(q.astype(F32) * n_st, axis=1, keepdims=True))
        hh = num / jnp.maximum(jnp.abs(den), jnp.exp(-m_t))
        b_last = bb_c[ln - 1:ln, :]
        wlog = b_last - bb_c + li_c
        m_new = jnp.maximum(b_last + m_st, jnp.max(wlog, axis=0, keepdims=True))
        decay = jnp.exp(b_last + m_st - m_new)
        ws = jnp.exp(wlog - m_new)
        wv = (ws * v.astype(F32)).astype(BF16)
        ct_ref[h] = decay * ct + lax.dot_general(k, wv, (((0,), (0,)), ((), ())),
                                                 preferred_element_type=F32)
        n_ref[h] = decay * n_st + jnp.sum(ws * k.astype(F32), axis=0, keepdims=True)
        m_ref[h] = jnp.broadcast_to(m_new, m_ref.shape[1:])
        heads.append(hh * lax.rsqrt(jnp.mean(hh * hh, axis=-1, keepdims=True) + EPS))
    hcat = jnp.concatenate(heads, axis=-1) * hnorm_ref[...]
    gated = (hcat * jax.nn.sigmoid(o_ref[rows, :].astype(F32))).astype(BF16)
    out_ref[rows, :] = x_ref[rows, :] + jnp.dot(gated, wout_ref[...], preferred_element_type=F32)


def _mlstm_layer(x, g, w_in, conv_w, conv_b, b_gates, head_norm, w_out, *, tm=512, tn=256,
                 chunk=256, core_tm=256):
    bsz, t, d = x.shape
    taps, qk_w = conv_w.shape
    v_w = head_norm.shape[0]
    dqk = qk_w // (2 * M_HEADS)
    dv = v_w // M_HEADS
    assert qk_w == v_w and qk_w % (2 * tn) == 0 and core_tm % chunk == 0
    w_gt = w_in[:, qk_w + 2 * v_w:].T.astype(BF16)
    w_in = w_in.astype(BF16)
    kern = functools.partial(_mlstm_proj_kernel, q_scale=dqk ** -0.5, tn=tn)
    qk, v, o, gt = pl.pallas_call(
        kern,
        grid=(bsz, t // tm),
        in_specs=[_row_tile(tm, d), _resident((1, d)), _resident(w_in.shape),
                  _resident((FGATE_ROWS, d)),
                  _resident(conv_w.shape), _resident((1, qk_w)), _resident((FGATE_ROWS, 1))],
        out_specs=[_row_tile(tm, qk_w), _row_tile(tm, v_w), _row_tile(tm, v_w),
                   pl.BlockSpec((None, FGATE_ROWS, tm), lambda b, i: (b, 0, i))],
        out_shape=[jax.ShapeDtypeStruct((bsz, t, qk_w), BF16),
                   jax.ShapeDtypeStruct((bsz, t, v_w), BF16),
                   jax.ShapeDtypeStruct((bsz, t, v_w), BF16),
                   jax.ShapeDtypeStruct((bsz, FGATE_ROWS, t), F32)],
        scratch_shapes=[pltpu.VMEM(((taps - 1) * SUBLANES, qk_w), F32), _slabs(tm, d),
                        _slabs(tm, qk_w)],
        compiler_params=_params(),
        name="mlstm_proj",
    )(x, g.reshape(1, d), w_in, w_gt, conv_w, conv_b.reshape(1, -1),
      b_gates.reshape(FGATE_ROWS, 1))

    kern = functools.partial(_mlstm_core_kernel, dqk=dqk, dv=dv, chunk=chunk)
    return pl.pallas_call(
        kern,
        grid=(bsz, t // core_tm),
        in_specs=[_row_tile(core_tm, d), _row_tile(core_tm, qk_w), _row_tile(core_tm, v_w),
                  _row_tile(core_tm, v_w),
                  pl.BlockSpec((None, FGATE_ROWS, core_tm), lambda b, i: (b, 0, i)),
                  _resident((1, v_w)), _resident((v_w, d))],
        out_specs=_row_tile(core_tm, d),
        out_shape=jax.ShapeDtypeStruct(x.shape, F32),
        scratch_shapes=[pltpu.VMEM((M_HEADS, dqk, dv), F32),
                        pltpu.VMEM((M_HEADS, 1, dqk), F32),
                        pltpu.VMEM((M_HEADS, 1, LANES), F32)],
        compiler_params=_params(),
        name="mlstm_core",
    )(x, qk, v, o, gt, head_norm.reshape(1, -1), w_out.astype(BF16))


def kernel(x, norm_mix, norm_ffn, norm_final, m_w_in, m_conv_w, m_conv_b, m_b_gates, m_head_norm, m_w_out, c_w_in, c_b_in, c_dw_w, c_dw_b, c_ln_g, c_ln_b, c_w_out, c_b_out, f_w_up, f_conv_w, f_conv_b, f_w_down):
    depth = norm_mix.shape[0]
    n_mixers = 2
    for i in range(depth):
        j = i // n_mixers
        if i % n_mixers == 0:
            x = _mlstm_layer(x, norm_mix[i], m_w_in[j], m_conv_w[j], m_conv_b[j], m_b_gates[j],
                             m_head_norm[j], m_w_out[j])
        else:
            x = _conformer(x, norm_mix[i], c_w_in[j], c_b_in[j], c_dw_w[j], c_dw_b[j],
                           c_ln_g[j], c_ln_b[j], c_w_out[j], c_b_out[j])
        x = _conv_ffn(x, norm_ffn[i], f_w_up[i], f_conv_w[i], f_conv_b[i], f_w_down[i],
                      norm_final, final_norm=(i == depth - 1))
    return x
```

```python
import functools

import jax
import jax.numpy as jnp
from jax import lax
from jax.experimental import pallas as pl
from jax.experimental.pallas import tpu as pltpu

EPS = 1e-6
M_HEADS = 4
FGATE_ROWS = 8
SUBLANES = 8
LANES = 128
VMEM_LIMIT_BYTES = 56 * 1024 * 1024

F32 = jnp.float32
BF16 = jnp.bfloat16


def _rms(x, g):
    return x * lax.rsqrt(jnp.mean(x * x, axis=-1, keepdims=True) + EPS) * g


def _resident(shape):
    return pl.BlockSpec(shape, lambda b, i: (0,) * len(shape), pipeline_mode=pl.Buffered(1))


def _row_tile(tm, width):
    return pl.BlockSpec((None, tm, width), lambda b, i: (b, i, 0))


def _params():
    return pltpu.CompilerParams(dimension_semantics=("arbitrary", "arbitrary"),
                                vmem_limit_bytes=VMEM_LIMIT_BYTES)


def _slabs(tm, width):
    return pltpu.VMEM((width // LANES, tm, LANES), F32)


def _permute_rows(slab_ref, val):
    tm = val.shape[0]
    a = tm // SUBLANES
    for s in range(slab_ref.shape[0]):
        for r in range(SUBLANES):
            slab_ref[s, pl.ds(r, a, stride=SUBLANES), :] = (
                val[r * a:(r + 1) * a, s * LANES:(s + 1) * LANES])


def _load_slabs(slab_ref):
    return jnp.concatenate([slab_ref[s] for s in range(slab_ref.shape[0])], axis=1)


def _store_slabs(slab_ref, val):
    for s in range(slab_ref.shape[0]):
        slab_ref[s] = val[:, s * LANES:(s + 1) * LANES]


def _time_rows(slab_ref, r):
    a = slab_ref.shape[1] // SUBLANES
    return jnp.concatenate([slab_ref[s, pl.ds(r, a, stride=SUBLANES), :]
                            for s in range(slab_ref.shape[0])], axis=1)


def _wrap_groups(u, halo_ref, cols, taps):
    tm, n = u.shape
    nh = (taps - 1) * SUBLANES
    sub = lax.broadcasted_iota(jnp.int32, (SUBLANES, n), 0)
    groups = []
    for g in range(taps - 1):
        rows = slice(tm - nh + g * SUBLANES, tm - nh + (g + 1) * SUBLANES)
        prev = halo_ref[g * SUBLANES:(g + 1) * SUBLANES, cols]
        groups.append(pltpu.roll(jnp.where(sub == SUBLANES - 1, prev, u[rows, :]), 1, axis=0))
    halo_ref[:, cols] = u[tm - nh:, :]
    return groups


def _causal_taps(u, halo_ref, cols, w):
    tm = u.shape[0]
    k = w.shape[0]
    ext = jnp.concatenate(_wrap_groups(u, halo_ref, cols, k) + [u], axis=0)
    y = w[k - 1:k, :] * u
    for j in range(k - 1):
        y = y + w[j:j + 1, :] * ext[j * SUBLANES:j * SUBLANES + tm, :]
    return y


def _ffn_kernel(x_ref, g_ref, wup_ref, cw_ref, cb_ref, wdn_ref, gfin_ref, o_ref,
                halo_ref, hn_ref, acc_ref, *, d_ff, tf, final_norm):
    @pl.when(pl.program_id(1) == 0)
    def _():
        halo_ref[...] = jnp.zeros_like(halo_ref)

    _permute_rows(hn_ref, _rms(x_ref[...], g_ref[...]))
    hn = _load_slabs(hn_ref).astype(BF16)

    bounds = [(lo, min(lo + tf, d_ff)) for lo in range(0, d_ff, tf)]

    def up(c):
        lo, hi = bounds[c]
        return [jnp.dot(hn, wup_ref[:, half * d_ff + lo:half * d_ff + hi],
                        preferred_element_type=F32) for half in range(2)]

    n_chunks = len(bounds)
    us_next = up(0)
    for c in range(n_chunks):
        us, us_next = us_next, (up(c + 1) if c + 1 < n_chunks else None)
        lo, hi = bounds[c]
        ys = []
        for half in range(2):
            cols = slice(half * d_ff + lo, half * d_ff + hi)
            ys.append(_causal_taps(us[half], halo_ref, cols, cw_ref[:, cols]) + cb_ref[:, cols])
        act = (ys[0] * jax.nn.sigmoid(ys[0]) * ys[1]).astype(BF16)
        contrib = jnp.dot(act, wdn_ref[lo:hi, :], preferred_element_type=F32)
        for s in range(acc_ref.shape[0]):
            piece = contrib[:, s * LANES:(s + 1) * LANES]
            if c == 0:
                acc_ref[s] = piece
            else:
                acc_ref[s] += piece
    a = x_ref.shape[0] // SUBLANES
    for r in range(SUBLANES):
        rows = slice(r * a, (r + 1) * a)
        out = x_ref[rows, :] + _time_rows(acc_ref, r)
        if final_norm:
            out = _rms(out, gfin_ref[...])
        o_ref[rows, :] = out


def _conv_ffn(x, g, w_up, conv_w, conv_b, w_down, g_final, *, final_norm, tm=512, tf=768):
    bsz, t, d = x.shape
    d_ff = w_down.shape[0]
    taps = conv_w.shape[0]
    kern = functools.partial(_ffn_kernel, d_ff=d_ff, tf=tf, final_norm=final_norm)
    return pl.pallas_call(
        kern,
        grid=(bsz, t // tm),
        in_specs=[_row_tile(tm, d), _resident((1, d)), _resident((d, 2 * d_ff)),
                  _resident(conv_w.shape), _resident((1, 2 * d_ff)), _resident((d_ff, d)),
                  _resident((1, d))],
        out_specs=_row_tile(tm, d),
        out_shape=jax.ShapeDtypeStruct(x.shape, F32),
        scratch_shapes=[pltpu.VMEM(((taps - 1) * SUBLANES, 2 * d_ff), F32),
                        _slabs(tm, d), _slabs(tm, d)],
        compiler_params=_params(),
        name="conv_ffn",
    )(x, g.reshape(1, d), w_up.astype(BF16), conv_w, conv_b.reshape(1, -1),
      w_down.astype(BF16), g_final.reshape(1, d))


def _conformer_kernel(x_ref, g_ref, win_ref, bin_ref, dw_ref, dwb_ref, lng_ref, lnb_ref,
                      wout_ref, bout_ref, o_ref, halo_ref, slab_ref, ext_ref, wb_ref, y_ref,
                      z_ref, *, block_rows, norm_rows):
    taps = dw_ref.shape[0]
    tm, d = x_ref.shape
    nh = (taps - 1) * SUBLANES

    @pl.when(pl.program_id(1) == 0)
    def _():
        halo_ref[...] = jnp.zeros_like(halo_ref)
        for j in range(taps):
            wb_ref[j] = jnp.broadcast_to(dw_ref[j:j + 1, :], (SUBLANES, d))

    _permute_rows(slab_ref, _rms(x_ref[...], g_ref[...]))
    hn = _load_slabs(slab_ref).astype(BF16)
    a1 = jnp.dot(hn, win_ref[:, 0:d], preferred_element_type=F32) + bin_ref[:, 0:d]
    a2 = jnp.dot(hn, win_ref[:, d:2 * d], preferred_element_type=F32) + bin_ref[:, d:2 * d]
    u = a1 * jax.nn.sigmoid(a2)
    wraps = _wrap_groups(u, halo_ref, slice(0, d), taps)
    for g, piece in enumerate(wraps):
        ext_ref[g * SUBLANES:(g + 1) * SUBLANES, :] = piece
    ext_ref[nh:nh + tm, :] = u

    groups = block_rows // SUBLANES
    for lb in range(d // LANES):
        lanes = slice(lb * LANES, (lb + 1) * LANES)
        ws = [wb_ref[j, :, lanes] for j in range(taps)]
        bias = jnp.broadcast_to(dwb_ref[:, lanes], (SUBLANES, LANES))

        def conv_block(i, carry, lanes=lanes, ws=ws, bias=bias):
            base = pl.multiple_of(i * block_rows, block_rows)
            accs = [bias] * groups
            for j in range(taps):
                for q in range(groups):
                    rows = pl.ds(base + (j + q) * SUBLANES, SUBLANES)
                    accs[q] = accs[q] + ws[j] * ext_ref[rows, lanes]
            y_ref[pl.ds(base, block_rows), lanes] = jnp.concatenate(accs, axis=0)
            return carry

        lax.fori_loop(0, tm // block_rows, conv_block, 0)

    def norm_block(i, carry):
        base = pl.multiple_of(i * norm_rows, norm_rows)
        y = y_ref[pl.ds(base, norm_rows), :]
        mu = jnp.mean(y, axis=-1, keepdims=True)
        yc = y - mu
        var = jnp.mean(yc * yc, axis=-1, keepdims=True)
        z = yc * lax.rsqrt(var + EPS) * lng_ref[...] + lnb_ref[...]
        z_ref[pl.ds(base, norm_rows), :] = (z * jax.nn.sigmoid(z)).astype(BF16)
        return carry

    lax.fori_loop(0, tm // norm_rows, norm_block, 0)
    res = jnp.dot(z_ref[...], wout_ref[...], preferred_element_type=F32) + bout_ref[...]
    _store_slabs(slab_ref, res)
    a = tm // SUBLANES
    for r in range(SUBLANES):
        rows = slice(r * a, (r + 1) * a)
        o_ref[rows, :] = x_ref[rows, :] + _time_rows(slab_ref, r)


def _conformer(x, g, w_in, b_in, dw_w, dw_b, ln_g, ln_b, w_out, b_out, *, tm=512,
               block_rows=128, norm_rows=256):
    bsz, t, d = x.shape
    k = dw_w.shape[0]
    nh = (k - 1) * SUBLANES
    kern = functools.partial(_conformer_kernel, block_rows=block_rows, norm_rows=norm_rows)
    vec = lambda a: a.reshape(1, -1)
    return pl.pallas_call(
        kern,
        grid=(bsz, t // tm),
        in_specs=[_row_tile(tm, d), _resident((1, d)), _resident((d, 2 * d)),
                  _resident((1, 2 * d)), _resident((k, d)), _resident((1, d)),
                  _resident((1, d)), _resident((1, d)), _resident((d, d)), _resident((1, d))],
        out_specs=_row_tile(tm, d),
        out_shape=jax.ShapeDtypeStruct(x.shape, F32),
        scratch_shapes=[pltpu.VMEM((nh, d), F32), _slabs(tm, d),
                        pltpu.VMEM((nh + tm, d), F32),
                        pltpu.VMEM((k, SUBLANES, d), F32),
                        pltpu.VMEM((tm, d), F32),
                        pltpu.VMEM((tm, d), BF16)],
        compiler_params=_params(),
        name="conformer",
    )(x, vec(g), w_in.astype(BF16), vec(b_in), dw_w, vec(dw_b), vec(ln_g), vec(ln_b),
      w_out.astype(BF16), vec(b_out))


def _mlstm_proj_kernel(x_ref, g_ref, win_ref, wgt_ref, cw_ref, cb_ref, bg_ref,
                       qk_ref, v_ref, o_ref, gt_ref, halo_ref, slab_ref, yslab_ref,
                       *, q_scale, tn):
    @pl.when(pl.program_id(1) == 0)
    def _():
        halo_ref[...] = jnp.zeros_like(halo_ref)

    hn32 = _rms(x_ref[...], g_ref[...])
    hn = hn32.astype(BF16)
    width = cw_ref.shape[1]
    _permute_rows(slab_ref, hn32)
    hn_perm = _load_slabs(slab_ref).astype(BF16)
    jobs = []
    for c in range(width // tn):
        jobs += [(hn_perm, 0, None, c), (hn, width, v_ref, c), (hn, 2 * width, o_ref, c)]
    a = x_ref.shape[0] // SUBLANES
    slabs_per_chunk = tn // LANES

    def project(job):
        lhs, col0, _, c = job
        return jnp.dot(lhs, win_ref[:, col0 + c * tn:col0 + (c + 1) * tn],
                       preferred_element_type=F32)

    nxt = project(jobs[0])
    for idx, (_, _, dst_ref, c) in enumerate(jobs):
        cur, nxt = nxt, (project(jobs[idx + 1]) if idx + 1 < len(jobs) else None)
        cols = slice(c * tn, (c + 1) * tn)
        if dst_ref is None:
            y = _causal_taps(cur, halo_ref, cols, cw_ref[:, cols]) + cb_ref[:, cols]
            y = y * jax.nn.sigmoid(y)
            if (c + 1) * tn <= width // 2:
                y = y * q_scale
            chunk_slabs = yslab_ref.at[c * slabs_per_chunk:(c + 1) * slabs_per_chunk]
            _store_slabs(chunk_slabs, y)
            for r in range(SUBLANES):
                qk_ref[r * a:(r + 1) * a, cols] = _time_rows(chunk_slabs, r).astype(BF16)
        else:
            dst_ref[:, cols] = cur.astype(BF16)
    gt = lax.dot_general(wgt_ref[...], hn, (((1,), (1,)), ((), ())),
                         preferred_element_type=F32) + bg_ref[...]
    row = lax.broadcasted_iota(jnp.int32, gt.shape, 0)
    gt_ref[...] = jnp.where(row < M_HEADS, gt, jax.nn.log_sigmoid(gt))


def _mlstm_core_kernel(x_ref, qk_ref, v_ref, o_ref, gt_ref, hnorm_ref, wout_ref, out_ref,
                       ct_ref, n_ref, m_ref, *, dqk, dv, chunk):
    @pl.when(pl.program_id(1) == 0)
    def _():
        ct_ref[...] = jnp.zeros_like(ct_ref)
        n_ref[...] = jnp.zeros_like(n_ref)
        m_ref[...] = jnp.zeros_like(m_ref)

    ln = chunk
    row = lax.broadcasted_iota(jnp.int32, (ln, ln), 0)
    col = lax.broadcasted_iota(jnp.int32, (ln, ln), 1)
    tri = col <= row
    eye = col == row
    n_batch = x_ref.shape[0]
    states = [[(ct_ref[bi, h], n_ref[bi, h], m_ref[bi, h][:, 0:1]) for h in range(M_HEADS)]
              for bi in range(n_batch)]
    for sub in range(x_ref.shape[1] // chunk):
        rows = slice(sub * chunk, (sub + 1) * chunk)
        for bi in range(n_batch):
            states[bi] = _mlstm_chunk(
                x_ref.at[bi], qk_ref.at[bi], v_ref.at[bi], o_ref.at[bi], gt_ref.at[bi],
                hnorm_ref, wout_ref, out_ref.at[bi], states[bi], rows, tri, eye, dqk, dv)
    for bi in range(n_batch):
        for h in range(M_HEADS):
            ct, n_st, m_st = states[bi][h]
            ct_ref[bi, h] = ct
            n_ref[bi, h] = n_st
            m_ref[bi, h] = jnp.broadcast_to(m_st, m_ref.shape[2:])


def _row_reduce(combine, reduce, x):
    acc = x[:, :LANES]
    for lb in range(1, x.shape[1] // LANES):
        acc = combine(acc, x[:, lb * LANES:(lb + 1) * LANES])
    return reduce(acc, axis=1, keepdims=True)


def _mlstm_chunk(x_ref, qk_ref, v_ref, o_ref, gt_ref, hnorm_ref, wout_ref, out_ref,
                 state, rows, tri, eye, dqk, dv):
    ln = rows.stop - rows.start
    heads = []
    new_state = []
    for h in range(M_HEADS):
        ct, n_st, m_st = state[h]
        q = qk_ref[rows, h * dqk:(h + 1) * dqk]
        k = qk_ref[rows, (M_HEADS + h) * dqk:(M_HEADS + h + 1) * dqk]
        v = v_ref[rows, h * dv:(h + 1) * dv]
        li_r = gt_ref[h:h + 1, rows]
        lf_r = gt_ref[M_HEADS + h:M_HEADS + h + 1, rows]
        bb_c = _row_reduce(jnp.add, jnp.sum, jnp.where(tri, lf_r, 0.0))
        bb_r = jnp.sum(jnp.where(eye, bb_c, 0.0), axis=0, keepdims=True)
        li_c = _row_reduce(jnp.add, jnp.sum, jnp.where(eye, li_r, 0.0))
        dmat = jnp.where(tri, bb_c + (li_r - bb_r), -jnp.inf)
        inter = bb_c + m_st
        m_t = jnp.maximum(inter, _row_reduce(jnp.maximum, jnp.max, dmat))
        s = lax.dot_general(q, k, (((1,), (1,)), ((), ())), preferred_element_type=F32)
        s = s * jnp.exp(dmat - m_t)
        sc = jnp.exp(inter - m_t)
        num = (jnp.dot(s.astype(BF16), v, preferred_element_type=F32)
               + sc * jnp.dot(q, ct.astype(BF16), preferred_element_type=F32))
        den = (_row_reduce(jnp.add, jnp.sum, s)
               + sc * jnp.sum(q.astype(F32) * n_st, axis=1, keepdims=True))
        hh = num / jnp.maximum(jnp.abs(den), jnp.exp(-m_t))
        b_last = bb_c[ln - 1:ln, :]
        wlog = b_last - bb_c + li_c
        m_new = jnp.maximum(b_last + m_st, jnp.max(wlog, axis=0, keepdims=True))
        decay = jnp.exp(b_last + m_st - m_new)
        ws = jnp.exp(wlog - m_new)
        wv = (ws * v.astype(F32)).astype(BF16)
        ct_new = decay * ct + lax.dot_general(k, wv, (((0,), (0,)), ((), ())),
                                              preferred_element_type=F32)
        n_new = decay * n_st + jnp.sum(ws * k.astype(F32), axis=0, keepdims=True)
        new_state.append((ct_new, n_new, m_new))
        heads.append(hh * lax.rsqrt(jnp.mean(hh * hh, axis=-1, keepdims=True) + EPS))
    hcat = jnp.concatenate(heads, axis=-1) * hnorm_ref[...]
    gated = (hcat * jax.nn.sigmoid(o_ref[rows, :].astype(F32))).astype(BF16)
    out_ref[rows, :] = x_ref[rows, :] + jnp.dot(gated, wout_ref[...], preferred_element_type=F32)
    return new_state


def _mlstm_layer(x, g, w_in, conv_w, conv_b, b_gates, head_norm, w_out, *, tm=512, tn=256,
                 chunk=128, core_tm=256, core_batch=2):
    bsz, t, d = x.shape
    assert bsz % core_batch == 0
    taps, qk_w = conv_w.shape
    v_w = head_norm.shape[0]
    dqk = qk_w // (2 * M_HEADS)
    dv = v_w // M_HEADS
    assert qk_w == v_w and qk_w % (2 * tn) == 0 and core_tm % chunk == 0
    w_gt = w_in[:, qk_w + 2 * v_w:].T.astype(BF16)
    w_in = w_in.astype(BF16)
    kern = functools.partial(_mlstm_proj_kernel, q_scale=dqk ** -0.5, tn=tn)
    qk, v, o, gt = pl.pallas_call(
        kern,
        grid=(bsz, t // tm),
        in_specs=[_row_tile(tm, d), _resident((1, d)), _resident(w_in.shape),
                  _resident((FGATE_ROWS, d)),
                  _resident(conv_w.shape), _resident((1, qk_w)), _resident((FGATE_ROWS, 1))],
        out_specs=[_row_tile(tm, qk_w), _row_tile(tm, v_w), _row_tile(tm, v_w),
                   pl.BlockSpec((None, FGATE_ROWS, tm), lambda b, i: (b, 0, i))],
        out_shape=[jax.ShapeDtypeStruct((bsz, t, qk_w), BF16),
                   jax.ShapeDtypeStruct((bsz, t, v_w), BF16),
                   jax.ShapeDtypeStruct((bsz, t, v_w), BF16),
                   jax.ShapeDtypeStruct((bsz, FGATE_ROWS, t), F32)],
        scratch_shapes=[pltpu.VMEM(((taps - 1) * SUBLANES, qk_w), F32), _slabs(tm, d),
                        _slabs(tm, qk_w)],
        compiler_params=_params(),
        name="mlstm_proj",
    )(x, g.reshape(1, d), w_in, w_gt, conv_w, conv_b.reshape(1, -1),
      b_gates.reshape(FGATE_ROWS, 1))

    kern = functools.partial(_mlstm_core_kernel, dqk=dqk, dv=dv, chunk=chunk)
    group = lambda width: pl.BlockSpec((core_batch, core_tm, width), lambda b, i: (b, i, 0))
    return pl.pallas_call(
        kern,
        grid=(bsz // core_batch, t // core_tm),
        in_specs=[group(d), group(qk_w), group(v_w), group(v_w),
                  pl.BlockSpec((core_batch, FGATE_ROWS, core_tm), lambda b, i: (b, 0, i)),
                  _resident((1, v_w)), _resident((v_w, d))],
        out_specs=group(d),
        out_shape=jax.ShapeDtypeStruct(x.shape, F32),
        scratch_shapes=[pltpu.VMEM((core_batch, M_HEADS, dqk, dv), F32),
                        pltpu.VMEM((core_batch, M_HEADS, 1, dqk), F32),
                        pltpu.VMEM((core_batch, M_HEADS, 1, LANES), F32)],
        compiler_params=_params(),
        name="mlstm_core",
    )(x, qk, v, o, gt, head_norm.reshape(1, -1), w_out.astype(BF16))


def kernel(x, norm_mix, norm_ffn, norm_final, m_w_in, m_conv_w, m_conv_b, m_b_gates, m_head_norm, m_w_out, c_w_in, c_b_in, c_dw_w, c_dw_b, c_ln_g, c_ln_b, c_w_out, c_b_out, f_w_up, f_conv_w, f_conv_b, f_w_down):
    depth = norm_mix.shape[0]
    n_mixers = 2
    for i in range(depth):
        j = i // n_mixers
        if i % n_mixers == 0:
            x = _mlstm_layer(x, norm_mix[i], m_w_in[j], m_conv_w[j], m_conv_b[j], m_b_gates[j],
                             m_head_norm[j], m_w_out[j])
        else:
            x = _conformer(x, norm_mix[i], c_w_in[j], c_b_in[j], c_dw_w[j], c_dw_b[j],
                           c_ln_g[j], c_ln_b[j], c_w_out[j], c_b_out[j])
        x = _conv_ffn(x, norm_ffn[i], f_w_up[i], f_conv_w[i], f_conv_b[i], f_w_down[i],
                      norm_final, final_norm=(i == depth - 1))
    return x
```

```python
import functools

import jax
import jax.numpy as jnp
from jax import lax
from jax.experimental import pallas as pl
from jax.experimental.pallas import tpu as pltpu

EPS = 1e-6
M_HEADS = 4
FGATE_ROWS = 8
SUBLANES = 8
LANES = 128
VMEM_LIMIT_BYTES = 56 * 1024 * 1024

F32 = jnp.float32
BF16 = jnp.bfloat16


def _rms(x, g):
    return x * lax.rsqrt(jnp.mean(x * x, axis=-1, keepdims=True) + EPS) * g


def _resident(shape):
    return pl.BlockSpec(shape, lambda b, i: (0,) * len(shape), pipeline_mode=pl.Buffered(1))


def _row_tile(tm, width):
    return pl.BlockSpec((None, tm, width), lambda b, i: (b, i, 0))


def _params():
    return pltpu.CompilerParams(dimension_semantics=("arbitrary", "arbitrary"),
                                vmem_limit_bytes=VMEM_LIMIT_BYTES)


def _slabs(tm, width):
    return pltpu.VMEM((width // LANES, tm, LANES), F32)


def _permute_rows(slab_ref, val):
    tm = val.shape[0]
    a = tm // SUBLANES
    for s in range(slab_ref.shape[0]):
        for r in range(SUBLANES):
            slab_ref[s, pl.ds(r, a, stride=SUBLANES), :] = (
                val[r * a:(r + 1) * a, s * LANES:(s + 1) * LANES])


def _load_slabs(slab_ref):
    return jnp.concatenate([slab_ref[s] for s in range(slab_ref.shape[0])], axis=1)


def _store_slabs(slab_ref, val):
    for s in range(slab_ref.shape[0]):
        slab_ref[s] = val[:, s * LANES:(s + 1) * LANES]


def _time_rows(slab_ref, r):
    a = slab_ref.shape[1] // SUBLANES
    return jnp.concatenate([slab_ref[s, pl.ds(r, a, stride=SUBLANES), :]
                            for s in range(slab_ref.shape[0])], axis=1)


def _wrap_groups(u, halo_ref, cols, taps):
    tm, n = u.shape
    nh = (taps - 1) * SUBLANES
    sub = lax.broadcasted_iota(jnp.int32, (SUBLANES, n), 0)
    groups = []
    for g in range(taps - 1):
        rows = slice(tm - nh + g * SUBLANES, tm - nh + (g + 1) * SUBLANES)
        prev = halo_ref[g * SUBLANES:(g + 1) * SUBLANES, cols]
        groups.append(pltpu.roll(jnp.where(sub == SUBLANES - 1, prev, u[rows, :]), 1, axis=0))
    halo_ref[:, cols] = u[tm - nh:, :]
    return groups


def _causal_taps(u, halo_ref, cols, w):
    tm = u.shape[0]
    k = w.shape[0]
    ext = jnp.concatenate(_wrap_groups(u, halo_ref, cols, k) + [u], axis=0)
    y = w[k - 1:k, :] * u
    for j in range(k - 1):
        y = y + w[j:j + 1, :] * ext[j * SUBLANES:j * SUBLANES + tm, :]
    return y


def _ffn_kernel(x_ref, g_ref, wup_ref, cw_ref, cb_ref, wdn_ref, gfin_ref, o_ref,
                halo_ref, hn_ref, acc_ref, *, d_ff, tf, final_norm):
    @pl.when(pl.program_id(1) == 0)
    def _():
        halo_ref[...] = jnp.zeros_like(halo_ref)

    _permute_rows(hn_ref, _rms(x_ref[...], g_ref[...]))
    hn = _load_slabs(hn_ref).astype(BF16)

    bounds = [(lo, min(lo + tf, d_ff)) for lo in range(0, d_ff, tf)]

    def up(c):
        lo, hi = bounds[c]
        return [jnp.dot(hn, wup_ref[:, half * d_ff + lo:half * d_ff + hi],
                        preferred_element_type=F32) for half in range(2)]

    n_chunks = len(bounds)
    us_next = up(0)
    for c in range(n_chunks):
        us, us_next = us_next, (up(c + 1) if c + 1 < n_chunks else None)
        lo, hi = bounds[c]
        ys = []
        for half in range(2):
            cols = slice(half * d_ff + lo, half * d_ff + hi)
            ys.append(_causal_taps(us[half], halo_ref, cols, cw_ref[:, cols]) + cb_ref[:, cols])
        act = (ys[0] * jax.nn.sigmoid(ys[0]) * ys[1]).astype(BF16)
        contrib = jnp.dot(act, wdn_ref[lo:hi, :], preferred_element_type=F32)
        for s in range(acc_ref.shape[0]):
            piece = contrib[:, s * LANES:(s + 1) * LANES]
            if c == 0:
                acc_ref[s] = piece
            else:
                acc_ref[s] += piece
    a = x_ref.shape[0] // SUBLANES
    for r in range(SUBLANES):
        rows = slice(r * a, (r + 1) * a)
        out = x_ref[rows, :] + _time_rows(acc_ref, r)
        if final_norm:
            out = _rms(out, gfin_ref[...])
        o_ref[rows, :] = out


def _conv_ffn(x, g, w_up, conv_w, conv_b, w_down, g_final, *, final_norm, tm=1024, tf=768):
    bsz, t, d = x.shape
    d_ff = w_down.shape[0]
    taps = conv_w.shape[0]
    kern = functools.partial(_ffn_kernel, d_ff=d_ff, tf=tf, final_norm=final_norm)
    return pl.pallas_call(
        kern,
        grid=(bsz, t // tm),
        in_specs=[_row_tile(tm, d), _resident((1, d)), _resident((d, 2 * d_ff)),
                  _resident(conv_w.shape), _resident((1, 2 * d_ff)), _resident((d_ff, d)),
                  _resident((1, d))],
        out_specs=_row_tile(tm, d),
        out_shape=jax.ShapeDtypeStruct(x.shape, F32),
        scratch_shapes=[pltpu.VMEM(((taps - 1) * SUBLANES, 2 * d_ff), F32),
                        _slabs(tm, d), _slabs(tm, d)],
        compiler_params=_params(),
        name="conv_ffn",
    )(x, g.reshape(1, d), w_up.astype(BF16), conv_w, conv_b.reshape(1, -1),
      w_down.astype(BF16), g_final.reshape(1, d))


def _conformer_kernel(x_ref, g_ref, win_ref, bin_ref, dw_ref, dwb_ref, lng_ref, lnb_ref,
                      wout_ref, bout_ref, o_ref, halo_ref, slab_ref, ext_ref, wb_ref, y_ref,
                      z_ref, *, block_rows, norm_rows):
    taps = dw_ref.shape[0]
    tm, d = x_ref.shape
    nh = (taps - 1) * SUBLANES

    @pl.when(pl.program_id(1) == 0)
    def _():
        halo_ref[...] = jnp.zeros_like(halo_ref)
        for j in range(taps):
            wb_ref[j] = jnp.broadcast_to(dw_ref[j:j + 1, :], (SUBLANES, d))

    _permute_rows(slab_ref, _rms(x_ref[...], g_ref[...]))
    hn = _load_slabs(slab_ref).astype(BF16)
    a1 = jnp.dot(hn, win_ref[:, 0:d], preferred_element_type=F32) + bin_ref[:, 0:d]
    a2 = jnp.dot(hn, win_ref[:, d:2 * d], preferred_element_type=F32) + bin_ref[:, d:2 * d]
    u = a1 * jax.nn.sigmoid(a2)
    wraps = _wrap_groups(u, halo_ref, slice(0, d), taps)
    for g, piece in enumerate(wraps):
        ext_ref[g * SUBLANES:(g + 1) * SUBLANES, :] = piece
    ext_ref[nh:nh + tm, :] = u

    groups = block_rows // SUBLANES
    for lb in range(d // LANES):
        lanes = slice(lb * LANES, (lb + 1) * LANES)
        ws = [wb_ref[j, :, lanes] for j in range(taps)]
        bias = jnp.broadcast_to(dwb_ref[:, lanes], (SUBLANES, LANES))

        def conv_block(i, carry, lanes=lanes, ws=ws, bias=bias):
            base = pl.multiple_of(i * block_rows, block_rows)
            accs = [bias] * groups
            for j in range(taps):
                for q in range(groups):
                    rows = pl.ds(base + (j + q) * SUBLANES, SUBLANES)
                    accs[q] = accs[q] + ws[j] * ext_ref[rows, lanes]
            y_ref[pl.ds(base, block_rows), lanes] = jnp.concatenate(accs, axis=0)
            return carry

        lax.fori_loop(0, tm // block_rows, conv_block, 0)

    def norm_block(i, carry):
        base = pl.multiple_of(i * norm_rows, norm_rows)
        y = y_ref[pl.ds(base, norm_rows), :]
        mu = jnp.mean(y, axis=-1, keepdims=True)
        yc = y - mu
        var = jnp.mean(yc * yc, axis=-1, keepdims=True)
        z = yc * lax.rsqrt(var + EPS) * lng_ref[...] + lnb_ref[...]
        z_ref[pl.ds(base, norm_rows), :] = (z * jax.nn.sigmoid(z)).astype(BF16)
        return carry

    lax.fori_loop(0, tm // norm_rows, norm_block, 0)
    res = jnp.dot(z_ref[...], wout_ref[...], preferred_element_type=F32) + bout_ref[...]
    _store_slabs(slab_ref, res)
    a = tm // SUBLANES
    for r in range(SUBLANES):
        rows = slice(r * a, (r + 1) * a)
        o_ref[rows, :] = x_ref[rows, :] + _time_rows(slab_ref, r)


def _conformer(x, g, w_in, b_in, dw_w, dw_b, ln_g, ln_b, w_out, b_out, *, tm=1024,
               block_rows=128, norm_rows=256):
    bsz, t, d = x.shape
    k = dw_w.shape[0]
    nh = (k - 1) * SUBLANES
    kern = functools.partial(_conformer_kernel, block_rows=block_rows, norm_rows=norm_rows)
    vec = lambda a: a.reshape(1, -1)
    return pl.pallas_call(
        kern,
        grid=(bsz, t // tm),
        in_specs=[_row_tile(tm, d), _resident((1, d)), _resident((d, 2 * d)),
                  _resident((1, 2 * d)), _resident((k, d)), _resident((1, d)),
                  _resident((1, d)), _resident((1, d)), _resident((d, d)), _resident((1, d))],
        out_specs=_row_tile(tm, d),
        out_shape=jax.ShapeDtypeStruct(x.shape, F32),
        scratch_shapes=[pltpu.VMEM((nh, d), F32), _slabs(tm, d),
                        pltpu.VMEM((nh + tm, d), F32),
                        pltpu.VMEM((k, SUBLANES, d), F32),
                        pltpu.VMEM((tm, d), F32),
                        pltpu.VMEM((tm, d), BF16)],
        compiler_params=_params(),
        name="conformer",
    )(x, vec(g), w_in.astype(BF16), vec(b_in), dw_w, vec(dw_b), vec(ln_g), vec(ln_b),
      w_out.astype(BF16), vec(b_out))


def _mlstm_proj_kernel(x_ref, g_ref, win_ref, wgt_ref, cw_ref, cb_ref, bg_ref,
                       qk_ref, v_ref, o_ref, gt_ref, halo_ref, slab_ref, yslab_ref,
                       *, q_scale, tn):
    @pl.when(pl.program_id(1) == 0)
    def _():
        halo_ref[...] = jnp.zeros_like(halo_ref)

    hn32 = _rms(x_ref[...], g_ref[...])
    hn = hn32.astype(BF16)
    width = cw_ref.shape[1]
    _permute_rows(slab_ref, hn32)
    hn_perm = _load_slabs(slab_ref).astype(BF16)
    jobs = []
    for c in range(width // tn):
        jobs += [(hn_perm, 0, None, c), (hn, width, v_ref, c), (hn, 2 * width, o_ref, c)]
    a = x_ref.shape[0] // SUBLANES
    slabs_per_chunk = tn // LANES

    def project(job):
        lhs, col0, _, c = job
        return jnp.dot(lhs, win_ref[:, col0 + c * tn:col0 + (c + 1) * tn],
                       preferred_element_type=F32)

    nxt = project(jobs[0])
    for idx, (_, _, dst_ref, c) in enumerate(jobs):
        cur, nxt = nxt, (project(jobs[idx + 1]) if idx + 1 < len(jobs) else None)
        cols = slice(c * tn, (c + 1) * tn)
        if dst_ref is None:
            y = _causal_taps(cur, halo_ref, cols, cw_ref[:, cols]) + cb_ref[:, cols]
            y = y * jax.nn.sigmoid(y)
            if (c + 1) * tn <= width // 2:
                y = y * q_scale
            chunk_slabs = yslab_ref.at[c * slabs_per_chunk:(c + 1) * slabs_per_chunk]
            _store_slabs(chunk_slabs, y)
            for r in range(SUBLANES):
                qk_ref[r * a:(r + 1) * a, cols] = _time_rows(chunk_slabs, r).astype(BF16)
        else:
            dst_ref[:, cols] = cur.astype(BF16)
    gt = lax.dot_general(wgt_ref[...], hn, (((1,), (1,)), ((), ())),
                         preferred_element_type=F32) + bg_ref[...]
    row = lax.broadcasted_iota(jnp.int32, gt.shape, 0)
    gt_ref[...] = jnp.where(row < M_HEADS, gt, jax.nn.log_sigmoid(gt))


def _mlstm_core_kernel(x_ref, qk_ref, v_ref, o_ref, gt_ref, hnorm_ref, wout_ref, out_ref,
                       ct_ref, n_ref, m_ref, *, dqk, dv, chunk):
    @pl.when(pl.program_id(1) == 0)
    def _():
        ct_ref[...] = jnp.zeros_like(ct_ref)
        n_ref[...] = jnp.zeros_like(n_ref)
        m_ref[...] = jnp.zeros_like(m_ref)

    ln = chunk
    row = lax.broadcasted_iota(jnp.int32, (ln, ln), 0)
    col = lax.broadcasted_iota(jnp.int32, (ln, ln), 1)
    tri = col <= row
    eye = col == row
    n_batch = x_ref.shape[0]
    states = [[(ct_ref[bi, h], n_ref[bi, h], m_ref[bi, h][:, 0:1]) for h in range(M_HEADS)]
              for bi in range(n_batch)]
    for sub in range(x_ref.shape[1] // chunk):
        rows = slice(sub * chunk, (sub + 1) * chunk)
        for bi in range(n_batch):
            states[bi] = _mlstm_chunk(
                x_ref.at[bi], qk_ref.at[bi], v_ref.at[bi], o_ref.at[bi], gt_ref.at[bi],
                hnorm_ref, wout_ref, out_ref.at[bi], states[bi], rows, tri, eye, dqk, dv)
    for bi in range(n_batch):
        for h in range(M_HEADS):
            ct, n_st, m_st = states[bi][h]
            ct_ref[bi, h] = ct
            n_ref[bi, h] = n_st
            m_ref[bi, h] = jnp.broadcast_to(m_st, m_ref.shape[2:])


def _row_reduce(combine, reduce, x):
    acc = x[:, :LANES]
    for lb in range(1, x.shape[1] // LANES):
        acc = combine(acc, x[:, lb * LANES:(lb + 1) * LANES])
    return reduce(acc, axis=1, keepdims=True)


def _mlstm_chunk(x_ref, qk_ref, v_ref, o_ref, gt_ref, hnorm_ref, wout_ref, out_ref,
                 state, rows, tri, eye, dqk, dv):
    ln = rows.stop - rows.start
    heads = []
    new_state = []
    for h in range(M_HEADS):
        ct, n_st, m_st = state[h]
        q = qk_ref[rows, h * dqk:(h + 1) * dqk]
        k = qk_ref[rows, (M_HEADS + h) * dqk:(M_HEADS + h + 1) * dqk]
        v = v_ref[rows, h * dv:(h + 1) * dv]
        li_r = gt_ref[h:h + 1, rows]
        lf_r = gt_ref[M_HEADS + h:M_HEADS + h + 1, rows]
        bb_c = _row_reduce(jnp.add, jnp.sum, jnp.where(tri, lf_r, 0.0))
        bb_r = jnp.sum(jnp.where(eye, bb_c, 0.0), axis=0, keepdims=True)
        li_c = _row_reduce(jnp.add, jnp.sum, jnp.where(eye, li_r, 0.0))
        dmat = jnp.where(tri, bb_c + (li_r - bb_r), -jnp.inf)
        inter = bb_c + m_st
        m_t = jnp.maximum(inter, _row_reduce(jnp.maximum, jnp.max, dmat))
        s = lax.dot_general(q, k, (((1,), (1,)), ((), ())), preferred_element_type=F32)
        s = s * jnp.exp(dmat - m_t)
        sc = jnp.exp(inter - m_t)
        num = (jnp.dot(s.astype(BF16), v, preferred_element_type=F32)
               + sc * jnp.dot(q, ct.astype(BF16), preferred_element_type=F32))
        den = (_row_reduce(jnp.add, jnp.sum, s)
               + sc * jnp.sum(q.astype(F32) * n_st, axis=1, keepdims=True))
        hh = num / jnp.maximum(jnp.abs(den), jnp.exp(-m_t))
        b_last = bb_c[ln - 1:ln, :]
        wlog = b_last - bb_c + li_c
        m_new = jnp.maximum(b_last + m_st, jnp.max(wlog, axis=0, keepdims=True))
        decay = jnp.exp(b_last + m_st - m_new)
        ws = jnp.exp(wlog - m_new)
        wv = (ws * v.astype(F32)).astype(BF16)
        ct_new = decay * ct + lax.dot_general(k, wv, (((0,), (0,)), ((), ())),
                                              preferred_element_type=F32)
        n_new = decay * n_st + jnp.sum(ws * k.astype(F32), axis=0, keepdims=True)
        new_state.append((ct_new, n_new, m_new))
        heads.append(hh * lax.rsqrt(jnp.mean(hh * hh, axis=-1, keepdims=True) + EPS))
    hcat = jnp.concatenate(heads, axis=-1) * hnorm_ref[...]
    gated = (hcat * jax.nn.sigmoid(o_ref[rows, :].astype(F32))).astype(BF16)
    out_ref[rows, :] = x_ref[rows, :] + jnp.dot(gated, wout_ref[...], preferred_element_type=F32)
    return new_state


def _mlstm_layer(x, g, w_in, conv_w, conv_b, b_gates, head_norm, w_out, *, tm=512, tn=256,
                 chunk=128, core_tm=256, core_batch=2):
    bsz, t, d = x.shape
    assert bsz % core_batch == 0
    taps, qk_w = conv_w.shape
    v_w = head_norm.shape[0]
    dqk = qk_w // (2 * M_HEADS)
    dv = v_w // M_HEADS
    assert qk_w == v_w and qk_w % (2 * tn) == 0 and core_tm % chunk == 0
    w_gt = w_in[:, qk_w + 2 * v_w:].T.astype(BF16)
    w_in = w_in.astype(BF16)
    kern = functools.partial(_mlstm_proj_kernel, q_scale=dqk ** -0.5, tn=tn)
    qk, v, o, gt = pl.pallas_call(
        kern,
        grid=(bsz, t // tm),
        in_specs=[_row_tile(tm, d), _resident((1, d)), _resident(w_in.shape),
                  _resident((FGATE_ROWS, d)),
                  _resident(conv_w.shape), _resident((1, qk_w)), _resident((FGATE_ROWS, 1))],
        out_specs=[_row_tile(tm, qk_w), _row_tile(tm, v_w), _row_tile(tm, v_w),
                   pl.BlockSpec((None, FGATE_ROWS, tm), lambda b, i: (b, 0, i))],
        out_shape=[jax.ShapeDtypeStruct((bsz, t, qk_w), BF16),
                   jax.ShapeDtypeStruct((bsz, t, v_w), BF16),
                   jax.ShapeDtypeStruct((bsz, t, v_w), BF16),
                   jax.ShapeDtypeStruct((bsz, FGATE_ROWS, t), F32)],
        scratch_shapes=[pltpu.VMEM(((taps - 1) * SUBLANES, qk_w), F32), _slabs(tm, d),
                        _slabs(tm, qk_w)],
        compiler_params=_params(),
        name="mlstm_proj",
    )(x, g.reshape(1, d), w_in, w_gt, conv_w, conv_b.reshape(1, -1),
      b_gates.reshape(FGATE_ROWS, 1))

    kern = functools.partial(_mlstm_core_kernel, dqk=dqk, dv=dv, chunk=chunk)
    group = lambda width: pl.BlockSpec((core_batch, core_tm, width), lambda b, i: (b, i, 0))
    return pl.pallas_call(
        kern,
        grid=(bsz // core_batch, t // core_tm),
        in_specs=[group(d), group(qk_w), group(v_w), group(v_w),
                  pl.BlockSpec((core_batch, FGATE_ROWS, core_tm), lambda b, i: (b, 0, i)),
                  _resident((1, v_w)), _resident((v_w, d))],
        out_specs=group(d),
        out_shape=jax.ShapeDtypeStruct(x.shape, F32),
        scratch_shapes=[pltpu.VMEM((core_batch, M_HEADS, dqk, dv), F32),
                        pltpu.VMEM((core_batch, M_HEADS, 1, dqk), F32),
                        pltpu.VMEM((core_batch, M_HEADS, 1, LANES), F32)],
        compiler_params=_params(),
        name="mlstm_core",
    )(x, qk, v, o, gt, head_norm.reshape(1, -1), w_out.astype(BF16))


def kernel(x, norm_mix, norm_ffn, norm_final, m_w_in, m_conv_w, m_conv_b, m_b_gates, m_head_norm, m_w_out, c_w_in, c_b_in, c_dw_w, c_dw_b, c_ln_g, c_ln_b, c_w_out, c_b_out, f_w_up, f_conv_w, f_conv_b, f_w_down):
    depth = norm_mix.shape[0]
    n_mixers = 2
    for i in range(depth):
        j = i // n_mixers
        if i % n_mixers == 0:
            x = _mlstm_layer(x, norm_mix[i], m_w_in[j], m_conv_w[j], m_conv_b[j], m_b_gates[j],
                             m_head_norm[j], m_w_out[j])
        else:
            x = _conformer(x, norm_mix[i], c_w_in[j], c_b_in[j], c_dw_w[j], c_dw_b[j],
                           c_ln_g[j], c_ln_b[j], c_w_out[j], c_b_out[j])
        x = _conv_ffn(x, norm_ffn[i], f_w_up[i], f_conv_w[i], f_conv_b[i], f_w_down[i],
                      norm_final, final_norm=(i == depth - 1))
    return x
```

```python
import functools

import jax
import jax.numpy as jnp
from jax import lax
from jax.experimental import pallas as pl
from jax.experimental.pallas import tpu as pltpu

EPS = 1e-6
M_HEADS = 4
FGATE_ROWS = 8
SUBLANES = 8
LANES = 128
VMEM_LIMIT_BYTES = 56 * 1024 * 1024

F32 = jnp.float32
BF16 = jnp.bfloat16


def _rms(x, g):
    return x * lax.rsqrt(jnp.mean(x * x, axis=-1, keepdims=True) + EPS) * g


def _resident(shape):
    return pl.BlockSpec(shape, lambda b, i: (0,) * len(shape), pipeline_mode=pl.Buffered(1))


def _row_tile(tm, width):
    return pl.BlockSpec((None, tm, width), lambda b, i: (b, i, 0))


def _params():
    return pltpu.CompilerParams(dimension_semantics=("arbitrary", "arbitrary"),
                                vmem_limit_bytes=VMEM_LIMIT_BYTES)


def _slabs(tm, width):
    return pltpu.VMEM((width // LANES, tm, LANES), F32)


def _permute_rows(slab_ref, val):
    tm = val.shape[0]
    a = tm // SUBLANES
    for s in range(slab_ref.shape[0]):
        for r in range(SUBLANES):
            slab_ref[s, pl.ds(r, a, stride=SUBLANES), :] = (
                val[r * a:(r + 1) * a, s * LANES:(s + 1) * LANES])


def _load_slabs(slab_ref):
    return jnp.concatenate([slab_ref[s] for s in range(slab_ref.shape[0])], axis=1)


def _store_slabs(slab_ref, val):
    for s in range(slab_ref.shape[0]):
        slab_ref[s] = val[:, s * LANES:(s + 1) * LANES]


def _time_rows(slab_ref, r):
    a = slab_ref.shape[1] // SUBLANES
    return jnp.concatenate([slab_ref[s, pl.ds(r, a, stride=SUBLANES), :]
                            for s in range(slab_ref.shape[0])], axis=1)


def _wrap_groups(u, halo_ref, cols, taps):
    tm, n = u.shape
    nh = (taps - 1) * SUBLANES
    sub = lax.broadcasted_iota(jnp.int32, (SUBLANES, n), 0)
    groups = []
    for g in range(taps - 1):
        rows = slice(tm - nh + g * SUBLANES, tm - nh + (g + 1) * SUBLANES)
        prev = halo_ref[g * SUBLANES:(g + 1) * SUBLANES, cols]
        groups.append(pltpu.roll(jnp.where(sub == SUBLANES - 1, prev, u[rows, :]), 1, axis=0))
    halo_ref[:, cols] = u[tm - nh:, :]
    return groups


def _causal_taps(u, halo_ref, cols, w):
    tm = u.shape[0]
    k = w.shape[0]
    ext = jnp.concatenate(_wrap_groups(u, halo_ref, cols, k) + [u], axis=0)
    y = w[k - 1:k, :] * u
    for j in range(k - 1):
        y = y + w[j:j + 1, :] * ext[j * SUBLANES:j * SUBLANES + tm, :]
    return y


def _ffn_kernel(x_ref, g_ref, wup_ref, cw_ref, cb_ref, wdn_ref, gfin_ref, o_ref,
                halo_ref, hn_ref, acc_ref, *, d_ff, tf, final_norm):
    @pl.when(pl.program_id(1) == 0)
    def _():
        halo_ref[...] = jnp.zeros_like(halo_ref)

    _permute_rows(hn_ref, _rms(x_ref[...], g_ref[...]))
    hn = _load_slabs(hn_ref).astype(BF16)

    bounds = [(lo, min(lo + tf, d_ff)) for lo in range(0, d_ff, tf)]

    def up(c):
        lo, hi = bounds[c]
        return [jnp.dot(hn, wup_ref[:, half * d_ff + lo:half * d_ff + hi],
                        preferred_element_type=F32) for half in range(2)]

    n_chunks = len(bounds)
    us_next = up(0)
    for c in range(n_chunks):
        us, us_next = us_next, (up(c + 1) if c + 1 < n_chunks else None)
        lo, hi = bounds[c]
        ys = []
        for half in range(2):
            cols = slice(half * d_ff + lo, half * d_ff + hi)
            ys.append(_causal_taps(us[half], halo_ref, cols, cw_ref[:, cols]) + cb_ref[:, cols])
        act = (ys[0] * jax.nn.sigmoid(ys[0]) * ys[1]).astype(BF16)
        contrib = jnp.dot(act, wdn_ref[lo:hi, :], preferred_element_type=F32)
        for s in range(acc_ref.shape[0]):
            piece = contrib[:, s * LANES:(s + 1) * LANES]
            if c == 0:
                acc_ref[s] = piece
            else:
                acc_ref[s] += piece
    a = x_ref.shape[0] // SUBLANES
    for r in range(SUBLANES):
        rows = slice(r * a, (r + 1) * a)
        out = x_ref[rows, :] + _time_rows(acc_ref, r)
        if final_norm:
            out = _rms(out, gfin_ref[...])
        o_ref[rows, :] = out


def _conv_ffn(x, g, w_up, conv_w, conv_b, w_down, g_final, *, final_norm, tm=1024, tf=768):
    bsz, t, d = x.shape
    d_ff = w_down.shape[0]
    taps = conv_w.shape[0]
    kern = functools.partial(_ffn_kernel, d_ff=d_ff, tf=tf, final_norm=final_norm)
    return pl.pallas_call(
        kern,
        grid=(bsz, t // tm),
        in_specs=[_row_tile(tm, d), _resident((1, d)), _resident((d, 2 * d_ff)),
                  _resident(conv_w.shape), _resident((1, 2 * d_ff)), _resident((d_ff, d)),
                  _resident((1, d))],
        out_specs=_row_tile(tm, d),
        out_shape=jax.ShapeDtypeStruct(x.shape, F32),
        scratch_shapes=[pltpu.VMEM(((taps - 1) * SUBLANES, 2 * d_ff), F32),
                        _slabs(tm, d), _slabs(tm, d)],
        compiler_params=_params(),
        name="conv_ffn",
    )(x, g.reshape(1, d), w_up.astype(BF16), conv_w, conv_b.reshape(1, -1),
      w_down.astype(BF16), g_final.reshape(1, d))


def _conformer_kernel(x_ref, g_ref, win_ref, bin_ref, dw_ref, dwb_ref, lng_ref, lnb_ref,
                      wout_ref, bout_ref, o_ref, halo_ref, slab_ref, ext_ref, wb_ref, y_ref,
                      z_ref, *, block_rows, norm_rows):
    taps = dw_ref.shape[0]
    tm, d = x_ref.shape
    nh = (taps - 1) * SUBLANES

    @pl.when(pl.program_id(1) == 0)
    def _():
        halo_ref[...] = jnp.zeros_like(halo_ref)
        for j in range(taps):
            wb_ref[j] = jnp.broadcast_to(dw_ref[j:j + 1, :], (SUBLANES, d))

    _permute_rows(slab_ref, _rms(x_ref[...], g_ref[...]))
    hn = _load_slabs(slab_ref).astype(BF16)
    a1 = jnp.dot(hn, win_ref[:, 0:d], preferred_element_type=F32) + bin_ref[:, 0:d]
    a2 = jnp.dot(hn, win_ref[:, d:2 * d], preferred_element_type=F32) + bin_ref[:, d:2 * d]
    u = a1 * jax.nn.sigmoid(a2)
    wraps = _wrap_groups(u, halo_ref, slice(0, d), taps)
    for g, piece in enumerate(wraps):
        ext_ref[g * SUBLANES:(g + 1) * SUBLANES, :] = piece
    ext_ref[nh:nh + tm, :] = u

    groups = block_rows // SUBLANES
    for lb in range(d // LANES):
        lanes = slice(lb * LANES, (lb + 1) * LANES)
        ws = [wb_ref[j, :, lanes] for j in range(taps)]
        bias = jnp.broadcast_to(dwb_ref[:, lanes], (SUBLANES, LANES))

        def conv_block(i, carry, lanes=lanes, ws=ws, bias=bias):
            base = pl.multiple_of(i * block_rows, block_rows)
            accs = [bias] * groups
            for j in range(taps):
                for q in range(groups):
                    rows = pl.ds(base + (j + q) * SUBLANES, SUBLANES)
                    accs[q] = accs[q] + ws[j] * ext_ref[rows, lanes]
            y_ref[pl.ds(base, block_rows), lanes] = jnp.concatenate(accs, axis=0)
            return carry

        lax.fori_loop(0, tm // block_rows, conv_block, 0)

    def norm_block(i, carry):
        base = pl.multiple_of(i * norm_rows, norm_rows)
        y = y_ref[pl.ds(base, norm_rows), :]
        mu = jnp.mean(y, axis=-1, keepdims=True)
        yc = y - mu
        var = jnp.mean(yc * yc, axis=-1, keepdims=True)
        z = yc * lax.rsqrt(var + EPS) * lng_ref[...] + lnb_ref[...]
        z_ref[pl.ds(base, norm_rows), :] = (z * jax.nn.sigmoid(z)).astype(BF16)
        return carry

    lax.fori_loop(0, tm // norm_rows, norm_block, 0)
    res = jnp.dot(z_ref[...], wout_ref[...], preferred_element_type=F32) + bout_ref[...]
    _store_slabs(slab_ref, res)
    a = tm // SUBLANES
    for r in range(SUBLANES):
        rows = slice(r * a, (r + 1) * a)
        o_ref[rows, :] = x_ref[rows, :] + _time_rows(slab_ref, r)


def _conformer(x, g, w_in, b_in, dw_w, dw_b, ln_g, ln_b, w_out, b_out, *, tm=1024,
               block_rows=128, norm_rows=256):
    bsz, t, d = x.shape
    k = dw_w.shape[0]
    nh = (k - 1) * SUBLANES
    kern = functools.partial(_conformer_kernel, block_rows=block_rows, norm_rows=norm_rows)
    vec = lambda a: a.reshape(1, -1)
    return pl.pallas_call(
        kern,
        grid=(bsz, t // tm),
        in_specs=[_row_tile(tm, d), _resident((1, d)), _resident((d, 2 * d)),
                  _resident((1, 2 * d)), _resident((k, d)), _resident((1, d)),
                  _resident((1, d)), _resident((1, d)), _resident((d, d)), _resident((1, d))],
        out_specs=_row_tile(tm, d),
        out_shape=jax.ShapeDtypeStruct(x.shape, F32),
        scratch_shapes=[pltpu.VMEM((nh, d), F32), _slabs(tm, d),
                        pltpu.VMEM((nh + tm, d), F32),
                        pltpu.VMEM((k, SUBLANES, d), F32),
                        pltpu.VMEM((tm, d), F32),
                        pltpu.VMEM((tm, d), BF16)],
        compiler_params=_params(),
        name="conformer",
    )(x, vec(g), w_in.astype(BF16), vec(b_in), dw_w, vec(dw_b), vec(ln_g), vec(ln_b),
      w_out.astype(BF16), vec(b_out))


def _mlstm_proj_kernel(x_ref, g_ref, win_ref, wgt_ref, cw_ref, cb_ref, bg_ref,
                       qk_ref, v_ref, o_ref, gt_ref, halo_ref, slab_ref, yslab_ref,
                       *, q_scale, tn):
    @pl.when(pl.program_id(1) == 0)
    def _():
        halo_ref[...] = jnp.zeros_like(halo_ref)

    hn32 = _rms(x_ref[...], g_ref[...])
    hn = hn32.astype(BF16)
    width = cw_ref.shape[1]
    _permute_rows(slab_ref, hn32)
    hn_perm = _load_slabs(slab_ref).astype(BF16)
    jobs = []
    for c in range(width // tn):
        jobs += [(hn_perm, 0, None, c), (hn, width, v_ref, c), (hn, 2 * width, o_ref, c)]
    a = x_ref.shape[0] // SUBLANES
    slabs_per_chunk = tn // LANES

    def project(job):
        lhs, col0, _, c = job
        return jnp.dot(lhs, win_ref[:, col0 + c * tn:col0 + (c + 1) * tn],
                       preferred_element_type=F32)

    nxt = project(jobs[0])
    for idx, (_, _, dst_ref, c) in enumerate(jobs):
        cur, nxt = nxt, (project(jobs[idx + 1]) if idx + 1 < len(jobs) else None)
        cols = slice(c * tn, (c + 1) * tn)
        if dst_ref is None:
            y = _causal_taps(cur, halo_ref, cols, cw_ref[:, cols]) + cb_ref[:, cols]
            y = y * jax.nn.sigmoid(y)
            if (c + 1) * tn <= width // 2:
                y = y * q_scale
            chunk_slabs = yslab_ref.at[c * slabs_per_chunk:(c + 1) * slabs_per_chunk]
            _store_slabs(chunk_slabs, y)
            for r in range(SUBLANES):
                qk_ref[r * a:(r + 1) * a, cols] = _time_rows(chunk_slabs, r).astype(BF16)
        else:
            dst_ref[:, cols] = cur.astype(BF16)
    gt = lax.dot_general(wgt_ref[...], hn, (((1,), (1,)), ((), ())),
                         preferred_element_type=F32) + bg_ref[...]
    row = lax.broadcasted_iota(jnp.int32, gt.shape, 0)
    gt_ref[...] = jnp.where(row < M_HEADS, gt, jax.nn.log_sigmoid(gt))


def _mlstm_core_kernel(x_ref, qk_ref, v_ref, o_ref, gt_ref, hnorm_ref, wout_ref, out_ref,
                       ct_ref, n_ref, m_ref, *, dqk, dv, chunk):
    @pl.when(pl.program_id(1) == 0)
    def _():
        ct_ref[...] = jnp.zeros_like(ct_ref)
        n_ref[...] = jnp.zeros_like(n_ref)
        m_ref[...] = jnp.zeros_like(m_ref)

    ln = chunk
    row = lax.broadcasted_iota(jnp.int32, (ln, ln), 0)
    col = lax.broadcasted_iota(jnp.int32, (ln, ln), 1)
    tri = col <= row
    eye = col == row
    n_batch = x_ref.shape[0]
    states = [[(ct_ref[bi, h], n_ref[bi, h], m_ref[bi, h][:, 0:1]) for h in range(M_HEADS)]
              for bi in range(n_batch)]
    for sub in range(x_ref.shape[1] // chunk):
        rows = slice(sub * chunk, (sub + 1) * chunk)
        for bi in range(n_batch):
            states[bi] = _mlstm_chunk(
                x_ref.at[bi], qk_ref.at[bi], v_ref.at[bi], o_ref.at[bi], gt_ref.at[bi],
                hnorm_ref, wout_ref, out_ref.at[bi], states[bi], rows, tri, eye, dqk, dv)
    for bi in range(n_batch):
        for h in range(M_HEADS):
            ct, n_st, m_st = states[bi][h]
            ct_ref[bi, h] = ct
            n_ref[bi, h] = n_st
            m_ref[bi, h] = jnp.broadcast_to(m_st, m_ref.shape[2:])


def _row_reduce(combine, reduce, x):
    acc = x[:, :LANES]
    for lb in range(1, x.shape[1] // LANES):
        acc = combine(acc, x[:, lb * LANES:(lb + 1) * LANES])
    return reduce(acc, axis=1, keepdims=True)


def _mlstm_chunk(x_ref, qk_ref, v_ref, o_ref, gt_ref, hnorm_ref, wout_ref, out_ref,
                 state, rows, tri, eye, dqk, dv):
    ln = rows.stop - rows.start
    heads = []
    new_state = []
    for h in range(M_HEADS):
        ct, n_st, m_st = state[h]
        q = qk_ref[rows, h * dqk:(h + 1) * dqk]
        k = qk_ref[rows, (M_HEADS + h) * dqk:(M_HEADS + h + 1) * dqk]
        v = v_ref[rows, h * dv:(h + 1) * dv]
        li_r = gt_ref[h:h + 1, rows]
        lf_r = gt_ref[M_HEADS + h:M_HEADS + h + 1, rows]
        bb_c = _row_reduce(jnp.add, jnp.sum, jnp.where(tri, lf_r, 0.0))
        bb_r = jnp.sum(jnp.where(eye, bb_c, 0.0), axis=0, keepdims=True)
        li_c = _row_reduce(jnp.add, jnp.sum, jnp.where(eye, li_r, 0.0))
        dmat = jnp.where(tri, bb_c + (li_r - bb_r), -jnp.inf)
        inter = bb_c + m_st
        m_t = jnp.maximum(inter, _row_reduce(jnp.maximum, jnp.max, dmat))
        s = lax.dot_general(q, k, (((1,), (1,)), ((), ())), preferred_element_type=F32)
        s = s * jnp.exp(dmat - m_t)
        sc = jnp.exp(inter - m_t)
        num = (jnp.dot(s.astype(BF16), v, preferred_element_type=F32)
               + sc * jnp.dot(q, ct.astype(BF16), preferred_element_type=F32))
        den = (_row_reduce(jnp.add, jnp.sum, s)
               + sc * jnp.sum(q.astype(F32) * n_st, axis=1, keepdims=True))
        hh = num / jnp.maximum(jnp.abs(den), jnp.exp(-m_t))
        b_last = bb_c[ln - 1:ln, :]
        wlog = b_last - bb_c + li_c
        m_new = jnp.maximum(b_last + m_st, jnp.max(wlog, axis=0, keepdims=True))
        decay = jnp.exp(b_last + m_st - m_new)
        ws = jnp.exp(wlog - m_new)
        wv = (ws * v.astype(F32)).astype(BF16)
        ct_new = decay * ct + lax.dot_general(k, wv, (((0,), (0,)), ((), ())),
                                              preferred_element_type=F32)
        n_new = decay * n_st + jnp.sum(ws * k.astype(F32), axis=0, keepdims=True)
        new_state.append((ct_new, n_new, m_new))
        heads.append(hh * lax.rsqrt(jnp.mean(hh * hh, axis=-1, keepdims=True) + EPS))
    hcat = jnp.concatenate(heads, axis=-1) * hnorm_ref[...]
    gated = (hcat * jax.nn.sigmoid(o_ref[rows, :].astype(F32))).astype(BF16)
    out_ref[rows, :] = x_ref[rows, :] + jnp.dot(gated, wout_ref[...], preferred_element_type=F32)
    return new_state


def _mlstm_layer(x, g, w_in, conv_w, conv_b, b_gates, head_norm, w_out, *, tm=1024, tn=256,
                 chunk=128, core_tm=256, core_batch=2):
    bsz, t, d = x.shape
    assert bsz % core_batch == 0
    taps, qk_w = conv_w.shape
    v_w = head_norm.shape[0]
    dqk = qk_w // (2 * M_HEADS)
    dv = v_w // M_HEADS
    assert qk_w == v_w and qk_w % (2 * tn) == 0 and core_tm % chunk == 0
    w_gt = w_in[:, qk_w + 2 * v_w:].T.astype(BF16)
    w_in = w_in.astype(BF16)
    kern = functools.partial(_mlstm_proj_kernel, q_scale=dqk ** -0.5, tn=tn)
    qk, v, o, gt = pl.pallas_call(
        kern,
        grid=(bsz, t // tm),
        in_specs=[_row_tile(tm, d), _resident((1, d)), _resident(w_in.shape),
                  _resident((FGATE_ROWS, d)),
                  _resident(conv_w.shape), _resident((1, qk_w)), _resident((FGATE_ROWS, 1))],
        out_specs=[_row_tile(tm, qk_w), _row_tile(tm, v_w), _row_tile(tm, v_w),
                   pl.BlockSpec((None, FGATE_ROWS, tm), lambda b, i: (b, 0, i))],
        out_shape=[jax.ShapeDtypeStruct((bsz, t, qk_w), BF16),
                   jax.ShapeDtypeStruct((bsz, t, v_w), BF16),
                   jax.ShapeDtypeStruct((bsz, t, v_w), BF16),
                   jax.ShapeDtypeStruct((bsz, FGATE_ROWS, t), F32)],
        scratch_shapes=[pltpu.VMEM(((taps - 1) * SUBLANES, qk_w), F32), _slabs(tm, d),
                        _slabs(tm, qk_w)],
        compiler_params=_params(),
        name="mlstm_proj",
    )(x, g.reshape(1, d), w_in, w_gt, conv_w, conv_b.reshape(1, -1),
      b_gates.reshape(FGATE_ROWS, 1))

    kern = functools.partial(_mlstm_core_kernel, dqk=dqk, dv=dv, chunk=chunk)
    group = lambda width: pl.BlockSpec((core_batch, core_tm, width), lambda b, i: (b, i, 0))
    return pl.pallas_call(
        kern,
        grid=(bsz // core_batch, t // core_tm),
        in_specs=[group(d), group(qk_w), group(v_w), group(v_w),
                  pl.BlockSpec((core_batch, FGATE_ROWS, core_tm), lambda b, i: (b, 0, i)),
                  _resident((1, v_w)), _resident((v_w, d))],
        out_specs=group(d),
        out_shape=jax.ShapeDtypeStruct(x.shape, F32),
        scratch_shapes=[pltpu.VMEM((core_batch, M_HEADS, dqk, dv), F32),
                        pltpu.VMEM((core_batch, M_HEADS, 1, dqk), F32),
                        pltpu.VMEM((core_batch, M_HEADS, 1, LANES), F32)],
        compiler_params=_params(),
        name="mlstm_core",
    )(x, qk, v, o, gt, head_norm.reshape(1, -1), w_out.astype(BF16))


def kernel(x, norm_mix, norm_ffn, norm_final, m_w_in, m_conv_w, m_conv_b, m_b_gates, m_head_norm, m_w_out, c_w_in, c_b_in, c_dw_w, c_dw_b, c_ln_g, c_ln_b, c_w_out, c_b_out, f_w_up, f_conv_w, f_conv_b, f_w_down):
    depth = norm_mix.shape[0]
    n_mixers = 2
    for i in range(depth):
        j = i // n_mixers
        if i % n_mixers == 0:
            x = _mlstm_layer(x, norm_mix[i], m_w_in[j], m_conv_w[j], m_conv_b[j], m_b_gates[j],
                             m_head_norm[j], m_w_out[j])
        else:
            x = _conformer(x, norm_mix[i], c_w_in[j], c_b_in[j], c_dw_w[j], c_dw_b[j],
                           c_ln_g[j], c_ln_b[j], c_w_out[j], c_b_out[j])
        x = _conv_ffn(x, norm_ffn[i], f_w_up[i], f_conv_w[i], f_conv_b[i], f_w_down[i],
                      norm_final, final_norm=(i == depth - 1))
    return x
```

```python
import functools

import jax
import jax.numpy as jnp
from jax import lax
from jax.experimental import pallas as pl
from jax.experimental.pallas import tpu as pltpu

EPS = 1e-6
M_HEADS = 4
FGATE_ROWS = 8
SUBLANES = 8
LANES = 128
VMEM_LIMIT_BYTES = 56 * 1024 * 1024

F32 = jnp.float32
BF16 = jnp.bfloat16


def _rms(x, g):
    return x * lax.rsqrt(jnp.mean(x * x, axis=-1, keepdims=True) + EPS) * g


def _resident(shape):
    return pl.BlockSpec(shape, lambda b, i: (0,) * len(shape), pipeline_mode=pl.Buffered(1))


def _row_tile(tm, width):
    return pl.BlockSpec((None, tm, width), lambda b, i: (b, i, 0))


def _params():
    return pltpu.CompilerParams(dimension_semantics=("arbitrary", "arbitrary"),
                                vmem_limit_bytes=VMEM_LIMIT_BYTES)


def _slabs(tm, width):
    return pltpu.VMEM((width // LANES, tm, LANES), F32)


def _permute_rows(slab_ref, val):
    tm = val.shape[0]
    a = tm // SUBLANES
    for s in range(slab_ref.shape[0]):
        for r in range(SUBLANES):
            slab_ref[s, pl.ds(r, a, stride=SUBLANES), :] = (
                val[r * a:(r + 1) * a, s * LANES:(s + 1) * LANES])


def _load_slabs(slab_ref):
    return jnp.concatenate([slab_ref[s] for s in range(slab_ref.shape[0])], axis=1)


def _store_slabs(slab_ref, val):
    for s in range(slab_ref.shape[0]):
        slab_ref[s] = val[:, s * LANES:(s + 1) * LANES]


def _time_rows(slab_ref, r):
    a = slab_ref.shape[1] // SUBLANES
    return jnp.concatenate([slab_ref[s, pl.ds(r, a, stride=SUBLANES), :]
                            for s in range(slab_ref.shape[0])], axis=1)


def _wrap_groups(u, halo_ref, cols, taps):
    tm, n = u.shape
    nh = (taps - 1) * SUBLANES
    sub = lax.broadcasted_iota(jnp.int32, (SUBLANES, n), 0)
    groups = []
    for g in range(taps - 1):
        rows = slice(tm - nh + g * SUBLANES, tm - nh + (g + 1) * SUBLANES)
        prev = halo_ref[g * SUBLANES:(g + 1) * SUBLANES, cols]
        groups.append(pltpu.roll(jnp.where(sub == SUBLANES - 1, prev, u[rows, :]), 1, axis=0))
    halo_ref[:, cols] = u[tm - nh:, :]
    return groups


def _causal_taps(u, halo_ref, cols, w):
    tm = u.shape[0]
    k = w.shape[0]
    ext = jnp.concatenate(_wrap_groups(u, halo_ref, cols, k) + [u], axis=0)
    y = w[k - 1:k, :] * u
    for j in range(k - 1):
        y = y + w[j:j + 1, :] * ext[j * SUBLANES:j * SUBLANES + tm, :]
    return y


def _ffn_kernel(x_ref, g_ref, wup_ref, cw_ref, cb_ref, wdn_ref, gfin_ref, o_ref,
                halo_ref, hn_ref, acc_ref, *, d_ff, tf, final_norm):
    @pl.when(pl.program_id(1) == 0)
    def _():
        halo_ref[...] = jnp.zeros_like(halo_ref)

    _permute_rows(hn_ref, _rms(x_ref[...], g_ref[...]))
    hn = _load_slabs(hn_ref).astype(BF16)

    bounds = [(lo, min(lo + tf, d_ff)) for lo in range(0, d_ff, tf)]

    def up(c):
        lo, hi = bounds[c]
        return [jnp.dot(hn, wup_ref[:, half * d_ff + lo:half * d_ff + hi],
                        preferred_element_type=F32) for half in range(2)]

    n_chunks = len(bounds)
    us_next = up(0)
    for c in range(n_chunks):
        us, us_next = us_next, (up(c + 1) if c + 1 < n_chunks else None)
        lo, hi = bounds[c]
        ys = []
        for half in range(2):
            cols = slice(half * d_ff + lo, half * d_ff + hi)
            ys.append(_causal_taps(us[half], halo_ref, cols, cw_ref[:, cols]) + cb_ref[:, cols])
        act = (ys[0] * jax.nn.sigmoid(ys[0]) * ys[1]).astype(BF16)
        contrib = jnp.dot(act, wdn_ref[lo:hi, :], preferred_element_type=F32)
        for s in range(acc_ref.shape[0]):
            piece = contrib[:, s * LANES:(s + 1) * LANES]
            if c == 0:
                acc_ref[s] = piece
            else:
                acc_ref[s] += piece
    a = x_ref.shape[0] // SUBLANES
    for r in range(SUBLANES):
        rows = slice(r * a, (r + 1) * a)
        out = x_ref[rows, :] + _time_rows(acc_ref, r)
        if final_norm:
            out = _rms(out, gfin_ref[...])
        o_ref[rows, :] = out


def _conv_ffn(x, g, w_up, conv_w, conv_b, w_down, g_final, *, final_norm, tm=1024, tf=768):
    bsz, t, d = x.shape
    d_ff = w_down.shape[0]
    taps = conv_w.shape[0]
    kern = functools.partial(_ffn_kernel, d_ff=d_ff, tf=tf, final_norm=final_norm)
    return pl.pallas_call(
        kern,
        grid=(bsz, t // tm),
        in_specs=[_row_tile(tm, d), _resident((1, d)), _resident((d, 2 * d_ff)),
                  _resident(conv_w.shape), _resident((1, 2 * d_ff)), _resident((d_ff, d)),
                  _resident((1, d))],
        out_specs=_row_tile(tm, d),
        out_shape=jax.ShapeDtypeStruct(x.shape, F32),
        scratch_shapes=[pltpu.VMEM(((taps - 1) * SUBLANES, 2 * d_ff), F32),
                        _slabs(tm, d), _slabs(tm, d)],
        compiler_params=_params(),
        name="conv_ffn",
    )(x, g.reshape(1, d), w_up.astype(BF16), conv_w, conv_b.reshape(1, -1),
      w_down.astype(BF16), g_final.reshape(1, d))


def _conformer_kernel(x_ref, g_ref, win_ref, bin_ref, dw_ref, dwb_ref, lng_ref, lnb_ref,
                      wout_ref, bout_ref, o_ref, halo_ref, slab_ref, ext_ref, wb_ref, y_ref,
                      z_ref, *, block_rows, norm_rows):
    taps = dw_ref.shape[0]
    tm, d = x_ref.shape
    nh = (taps - 1) * SUBLANES

    @pl.when(pl.program_id(1) == 0)
    def _():
        halo_ref[...] = jnp.zeros_like(halo_ref)
        for j in range(taps):
            wb_ref[j] = jnp.broadcast_to(dw_ref[j:j + 1, :], (SUBLANES, d))

    _permute_rows(slab_ref, _rms(x_ref[...], g_ref[...]))
    hn = _load_slabs(slab_ref).astype(BF16)
    a1 = jnp.dot(hn, win_ref[:, 0:d], preferred_element_type=F32) + bin_ref[:, 0:d]
    a2 = jnp.dot(hn, win_ref[:, d:2 * d], preferred_element_type=F32) + bin_ref[:, d:2 * d]
    u = a1 * jax.nn.sigmoid(a2)
    wraps = _wrap_groups(u, halo_ref, slice(0, d), taps)
    for g, piece in enumerate(wraps):
        ext_ref[g * SUBLANES:(g + 1) * SUBLANES, :] = piece
    ext_ref[nh:nh + tm, :] = u

    groups = block_rows // SUBLANES
    for lb in range(d // LANES):
        lanes = slice(lb * LANES, (lb + 1) * LANES)
        ws = [wb_ref[j, :, lanes] for j in range(taps)]
        bias = jnp.broadcast_to(dwb_ref[:, lanes], (SUBLANES, LANES))

        def conv_block(i, carry, lanes=lanes, ws=ws, bias=bias):
            base = pl.multiple_of(i * block_rows, block_rows)
            accs = [bias] * groups
            for j in range(taps):
                for q in range(groups):
                    rows = pl.ds(base + (j + q) * SUBLANES, SUBLANES)
                    accs[q] = accs[q] + ws[j] * ext_ref[rows, lanes]
            y_ref[pl.ds(base, block_rows), lanes] = jnp.concatenate(accs, axis=0)
            return carry

        lax.fori_loop(0, tm // block_rows, conv_block, 0)

    def norm_block(i, carry):
        base = pl.multiple_of(i * norm_rows, norm_rows)
        y = y_ref[pl.ds(base, norm_rows), :]
        mu = jnp.mean(y, axis=-1, keepdims=True)
        yc = y - mu
        var = jnp.mean(yc * yc, axis=-1, keepdims=True)
        z = yc * lax.rsqrt(var + EPS) * lng_ref[...] + lnb_ref[...]
        z_ref[pl.ds(base, norm_rows), :] = (z * jax.nn.sigmoid(z)).astype(BF16)
        return carry

    lax.fori_loop(0, tm // norm_rows, norm_block, 0)
    res = jnp.dot(z_ref[...], wout_ref[...], preferred_element_type=F32) + bout_ref[...]
    _store_slabs(slab_ref, res)
    a = tm // SUBLANES
    for r in range(SUBLANES):
        rows = slice(r * a, (r + 1) * a)
        o_ref[rows, :] = x_ref[rows, :] + _time_rows(slab_ref, r)


def _conformer(x, g, w_in, b_in, dw_w, dw_b, ln_g, ln_b, w_out, b_out, *, tm=1024,
               block_rows=128, norm_rows=256):
    bsz, t, d = x.shape
    k = dw_w.shape[0]
    nh = (k - 1) * SUBLANES
    kern = functools.partial(_conformer_kernel, block_rows=block_rows, norm_rows=norm_rows)
    vec = lambda a: a.reshape(1, -1)
    return pl.pallas_call(
        kern,
        grid=(bsz, t // tm),
        in_specs=[_row_tile(tm, d), _resident((1, d)), _resident((d, 2 * d)),
                  _resident((1, 2 * d)), _resident((k, d)), _resident((1, d)),
                  _resident((1, d)), _resident((1, d)), _resident((d, d)), _resident((1, d))],
        out_specs=_row_tile(tm, d),
        out_shape=jax.ShapeDtypeStruct(x.shape, F32),
        scratch_shapes=[pltpu.VMEM((nh, d), F32), _slabs(tm, d),
                        pltpu.VMEM((nh + tm, d), F32),
                        pltpu.VMEM((k, SUBLANES, d), F32),
                        pltpu.VMEM((tm, d), F32),
                        pltpu.VMEM((tm, d), BF16)],
        compiler_params=_params(),
        name="conformer",
    )(x, vec(g), w_in.astype(BF16), vec(b_in), dw_w, vec(dw_b), vec(ln_g), vec(ln_b),
      w_out.astype(BF16), vec(b_out))


def _mlstm_proj_kernel(x_ref, g_ref, win_ref, wgt_ref, cw_ref, cb_ref, bg_ref,
                       qk_ref, v_ref, o_ref, gt_ref, halo_ref, slab_ref, yslab_ref,
                       *, q_scale, tn):
    @pl.when(pl.program_id(1) == 0)
    def _():
        halo_ref[...] = jnp.zeros_like(halo_ref)

    hn32 = _rms(x_ref[...], g_ref[...])
    hn = hn32.astype(BF16)
    width = cw_ref.shape[1]
    _permute_rows(slab_ref, hn32)
    hn_perm = _load_slabs(slab_ref).astype(BF16)
    jobs = []
    for c in range(width // tn):
        jobs += [(hn_perm, 0, None, c), (hn, width, v_ref, c), (hn, 2 * width, o_ref, c)]
    a = x_ref.shape[0] // SUBLANES
    slabs_per_chunk = tn // LANES

    def project(job):
        lhs, col0, _, c = job
        return jnp.dot(lhs, win_ref[:, col0 + c * tn:col0 + (c + 1) * tn],
                       preferred_element_type=F32)

    nxt = project(jobs[0])
    for idx, (_, _, dst_ref, c) in enumerate(jobs):
        cur, nxt = nxt, (project(jobs[idx + 1]) if idx + 1 < len(jobs) else None)
        cols = slice(c * tn, (c + 1) * tn)
        if dst_ref is None:
            y = _causal_taps(cur, halo_ref, cols, cw_ref[:, cols]) + cb_ref[:, cols]
            y = y * jax.nn.sigmoid(y)
            if (c + 1) * tn <= width // 2:
                y = y * q_scale
            chunk_slabs = yslab_ref.at[c * slabs_per_chunk:(c + 1) * slabs_per_chunk]
            _store_slabs(chunk_slabs, y)
            for r in range(SUBLANES):
                qk_ref[r * a:(r + 1) * a, cols] = _time_rows(chunk_slabs, r).astype(BF16)
        else:
            dst_ref[:, cols] = cur.astype(BF16)
    gt = lax.dot_general(wgt_ref[...], hn, (((1,), (1,)), ((), ())),
                         preferred_element_type=F32) + bg_ref[...]
    row = lax.broadcasted_iota(jnp.int32, gt.shape, 0)
    gt_ref[...] = jnp.where(row < M_HEADS, gt, jax.nn.log_sigmoid(gt))


def _mlstm_core_kernel(x_ref, qk_ref, v_ref, o_ref, gt_ref, hnorm_ref, wout_ref, out_ref,
                       ct_ref, n_ref, m_ref, *, dqk, dv, chunk):
    @pl.when(pl.program_id(1) == 0)
    def _():
        ct_ref[...] = jnp.zeros_like(ct_ref)
        n_ref[...] = jnp.zeros_like(n_ref)
        m_ref[...] = jnp.zeros_like(m_ref)

    ln = chunk
    row = lax.broadcasted_iota(jnp.int32, (ln, ln), 0)
    col = lax.broadcasted_iota(jnp.int32, (ln, ln), 1)
    tri = col <= row
    eye = col == row
    n_batch = x_ref.shape[0]
    states = [[(ct_ref[bi, h], n_ref[bi, h], m_ref[bi, h][:, 0:1]) for h in range(M_HEADS)]
              for bi in range(n_batch)]
    for sub in range(x_ref.shape[1] // chunk):
        rows = slice(sub * chunk, (sub + 1) * chunk)
        for bi in range(n_batch):
            states[bi] = _mlstm_chunk(
                x_ref.at[bi], qk_ref.at[bi], v_ref.at[bi], o_ref.at[bi], gt_ref.at[bi],
                hnorm_ref, wout_ref, out_ref.at[bi], states[bi], rows, tri, eye, dqk, dv)
    for bi in range(n_batch):
        for h in range(M_HEADS):
            ct, n_st, m_st = states[bi][h]
            ct_ref[bi, h] = ct
            n_ref[bi, h] = n_st
            m_ref[bi, h] = jnp.broadcast_to(m_st, m_ref.shape[2:])


def _row_reduce(combine, reduce, x):
    acc = x[:, :LANES]
    for lb in range(1, x.shape[1] // LANES):
        acc = combine(acc, x[:, lb * LANES:(lb + 1) * LANES])
    return reduce(acc, axis=1, keepdims=True)


def _mlstm_chunk(x_ref, qk_ref, v_ref, o_ref, gt_ref, hnorm_ref, wout_ref, out_ref,
                 state, rows, tri, eye, dqk, dv):
    ln = rows.stop - rows.start
    heads = []
    new_state = []
    for h in range(M_HEADS):
        ct, n_st, m_st = state[h]
        q = qk_ref[rows, h * dqk:(h + 1) * dqk]
        k = qk_ref[rows, (M_HEADS + h) * dqk:(M_HEADS + h + 1) * dqk]
        v = v_ref[rows, h * dv:(h + 1) * dv]
        li_r = gt_ref[h:h + 1, rows]
        lf_r = gt_ref[M_HEADS + h:M_HEADS + h + 1, rows]
        bb_c = _row_reduce(jnp.add, jnp.sum, jnp.where(tri, lf_r, 0.0))
        bb_r = jnp.sum(jnp.where(eye, bb_c, 0.0), axis=0, keepdims=True)
        li_c = _row_reduce(jnp.add, jnp.sum, jnp.where(eye, li_r, 0.0))
        dmat = jnp.where(tri, bb_c + (li_r - bb_r), -jnp.inf)
        inter = bb_c + m_st
        m_t = jnp.maximum(inter, _row_reduce(jnp.maximum, jnp.max, dmat))
        s = lax.dot_general(q, k, (((1,), (1,)), ((), ())), preferred_element_type=F32)
        s = s * jnp.exp(dmat - m_t)
        sc = jnp.exp(inter - m_t)
        num = (jnp.dot(s.astype(BF16), v, preferred_element_type=F32)
               + sc * jnp.dot(q, ct.astype(BF16), preferred_element_type=F32))
        den = (_row_reduce(jnp.add, jnp.sum, s)
               + sc * jnp.sum(q.astype(F32) * n_st, axis=1, keepdims=True))
        hh = num / jnp.maximum(jnp.abs(den), jnp.exp(-m_t))
        b_last = bb_c[ln - 1:ln, :]
        wlog = b_last - bb_c + li_c
        m_new = jnp.maximum(b_last + m_st, jnp.max(wlog, axis=0, keepdims=True))
        decay = jnp.exp(b_last + m_st - m_new)
        ws = jnp.exp(wlog - m_new)
        wv = (ws * v.astype(F32)).astype(BF16)
        ct_new = decay * ct + lax.dot_general(k, wv, (((0,), (0,)), ((), ())),
                                              preferred_element_type=F32)
        n_new = decay * n_st + jnp.sum(ws * k.astype(F32), axis=0, keepdims=True)
        new_state.append((ct_new, n_new, m_new))
        heads.append(hh * lax.rsqrt(jnp.mean(hh * hh, axis=-1, keepdims=True) + EPS))
    hcat = jnp.concatenate(heads, axis=-1) * hnorm_ref[...]
    gated = (hcat * jax.nn.sigmoid(o_ref[rows, :].astype(F32))).astype(BF16)
    out_ref[rows, :] = x_ref[rows, :] + jnp.dot(gated, wout_ref[...], preferred_element_type=F32)
    return new_state


def _mlstm_layer(x, g, w_in, conv_w, conv_b, b_gates, head_norm, w_out, *, tm=1024, tn=256,
                 chunk=128, core_tm=256, core_batch=4):
    bsz, t, d = x.shape
    assert bsz % core_batch == 0
    taps, qk_w = conv_w.shape
    v_w = head_norm.shape[0]
    dqk = qk_w // (2 * M_HEADS)
    dv = v_w // M_HEADS
    assert qk_w == v_w and qk_w % (2 * tn) == 0 and core_tm % chunk == 0
    w_gt = w_in[:, qk_w + 2 * v_w:].T.astype(BF16)
    w_in = w_in.astype(BF16)
    kern = functools.partial(_mlstm_proj_kernel, q_scale=dqk ** -0.5, tn=tn)
    qk, v, o, gt = pl.pallas_call(
        kern,
        grid=(bsz, t // tm),
        in_specs=[_row_tile(tm, d), _resident((1, d)), _resident(w_in.shape),
                  _resident((FGATE_ROWS, d)),
                  _resident(conv_w.shape), _resident((1, qk_w)), _resident((FGATE_ROWS, 1))],
        out_specs=[_row_tile(tm, qk_w), _row_tile(tm, v_w), _row_tile(tm, v_w),
                   pl.BlockSpec((None, FGATE_ROWS, tm), lambda b, i: (b, 0, i))],
        out_shape=[jax.ShapeDtypeStruct((bsz, t, qk_w), BF16),
                   jax.ShapeDtypeStruct((bsz, t, v_w), BF16),
                   jax.ShapeDtypeStruct((bsz, t, v_w), BF16),
                   jax.ShapeDtypeStruct((bsz, FGATE_ROWS, t), F32)],
        scratch_shapes=[pltpu.VMEM(((taps - 1) * SUBLANES, qk_w), F32), _slabs(tm, d),
                        _slabs(tm, qk_w)],
        compiler_params=_params(),
        name="mlstm_proj",
    )(x, g.reshape(1, d), w_in, w_gt, conv_w, conv_b.reshape(1, -1),
      b_gates.reshape(FGATE_ROWS, 1))

    kern = functools.partial(_mlstm_core_kernel, dqk=dqk, dv=dv, chunk=chunk)
    group = lambda width: pl.BlockSpec((core_batch, core_tm, width), lambda b, i: (b, i, 0))
    return pl.pallas_call(
        kern,
        grid=(bsz // core_batch, t // core_tm),
        in_specs=[group(d), group(qk_w), group(v_w), group(v_w),
                  pl.BlockSpec((core_batch, FGATE_ROWS, core_tm), lambda b, i: (b, 0, i)),
                  _resident((1, v_w)), _resident((v_w, d))],
        out_specs=group(d),
        out_shape=jax.ShapeDtypeStruct(x.shape, F32),
        scratch_shapes=[pltpu.VMEM((core_batch, M_HEADS, dqk, dv), F32),
                        pltpu.VMEM((core_batch, M_HEADS, 1, dqk), F32),
                        pltpu.VMEM((core_batch, M_HEADS, 1, LANES), F32)],
        compiler_params=_params(),
        name="mlstm_core",
    )(x, qk, v, o, gt, head_norm.reshape(1, -1), w_out.astype(BF16))


def kernel(x, norm_mix, norm_ffn, norm_final, m_w_in, m_conv_w, m_conv_b, m_b_gates, m_head_norm, m_w_out, c_w_in, c_b_in, c_dw_w, c_dw_b, c_ln_g, c_ln_b, c_w_out, c_b_out, f_w_up, f_conv_w, f_conv_b, f_w_down):
    depth = norm_mix.shape[0]
    n_mixers = 2
    for i in range(depth):
        j = i // n_mixers
        if i % n_mixers == 0:
            x = _mlstm_layer(x, norm_mix[i], m_w_in[j], m_conv_w[j], m_conv_b[j], m_b_gates[j],
                             m_head_norm[j], m_w_out[j])
        else:
            x = _conformer(x, norm_mix[i], c_w_in[j], c_b_in[j], c_dw_w[j], c_dw_b[j],
                           c_ln_g[j], c_ln_b[j], c_w_out[j], c_b_out[j])
        x = _conv_ffn(x, norm_ffn[i], f_w_up[i], f_conv_w[i], f_conv_b[i], f_w_down[i],
                      norm_final, final_norm=(i == depth - 1))
    return x
```
